```python
import jax, jax.numpy as jnp
from jax import lax
import numpy as np


D_MODEL = 4096
BATCH = 2
SEQ = 4096
DEPTH = 2
DEC_BATCH = 8
DEC_SEQ = 16
PAST_LEN = 1024

CHUNK = 64
N_MIXERS = 2
N_POOL_LAYERS = (DEPTH + 1) // 2
N_CONV_LAYERS = DEPTH // 2
POOL_WINDOWS = (2, 4, 8, 16)
N_POOL_GROUPS = len(POOL_WINDOWS)
POOL_GROUP = D_MODEL // N_POOL_GROUPS
POOL_HIST = max(POOL_WINDOWS) - 1
CONV_WIDTH = 31
CONV_HIST = CONV_WIDTH - 1
D_FF = 256 * ((8 * D_MODEL // 3 + 255) // 256)
PLE_DIM = 256
EPS = 1e-6

kernel_name = 'streaming_pool_conformer_macaron_ple'


def _rmsnorm(x, g):
    xf = x.astype(jnp.float32)
    r = lax.rsqrt(jnp.mean(xf * xf, axis=-1, keepdims=True) + EPS)
    return (xf * r).astype(x.dtype) * g


def _layernorm(x, g, b):
    xf = x.astype(jnp.float32)
    mu = jnp.mean(xf, axis=-1, keepdims=True)
    var = jnp.mean(jnp.square(xf - mu), axis=-1, keepdims=True)
    return ((xf - mu) * lax.rsqrt(var + EPS)).astype(x.dtype) * g + b


def _swiglu(h, w_gate, w_up, w_down):
    return (jax.nn.silu(h @ w_gate) * (h @ w_up)) @ w_down


def _pool_mixer(h, hist, start_pos, w, b, scale):
    B, n, _ = h.shape
    h_ext = jnp.concatenate([hist, h], axis=1)
    cs = jnp.cumsum(h_ext.astype(jnp.float32), axis=1)
    cs = jnp.pad(cs, ((0, 0), (1, 0), (0, 0)))
    pos = start_pos + jnp.arange(n)
    means = []
    for g, win in enumerate(POOL_WINDOWS):
        sl = slice(g * POOL_GROUP, (g + 1) * POOL_GROUP)
        lo = POOL_HIST + 1 - win
        s = cs[:, POOL_HIST + 1:, sl] - cs[:, lo:lo + n, sl]
        cnt = jnp.minimum(pos + 1, win).astype(jnp.float32)
        means.append(s / cnt[None, :, None])
    mean = jnp.concatenate(means, axis=-1).astype(h.dtype)
    d = (mean - h).reshape(B, n, N_POOL_GROUPS, POOL_GROUP)
    y = jnp.einsum('bngc,gcd->bngd', d, w) + b
    y = y.reshape(B, n, D_MODEL) * scale
    return y, h_ext[:, -POOL_HIST:]


def _conv_module(h, hist, w_pw1, b_pw1, w_dw, b_dw, ln_g, ln_b, w_pw2, b_pw2):
    a = h @ w_pw1 + b_pw1
    u = a[..., :D_MODEL] * jax.nn.sigmoid(a[..., D_MODEL:])
    u_ext = jnp.concatenate([hist, u], axis=1)
    c = lax.conv_general_dilated(
        u_ext, w_dw[:, None, :], window_strides=(1,), padding='VALID',
        dimension_numbers=('NWC', 'WIO', 'NWC'), feature_group_count=D_MODEL) + b_dw
    c = jax.nn.silu(_layernorm(c, ln_g, ln_b))
    y = c @ w_pw2 + b_pw2
    return y, u_ext[:, -CONV_HIST:]


def _trunk(x, p, pool_hist, conv_hist, start_pos, W):
    new_pool, new_conv = [], []
    for i in range(DEPTH):
        h = _rmsnorm(x, W['g_ffn1'][i])
        x = x + 0.5 * _swiglu(h, W['ffn1_w_gate'][i], W['ffn1_w_up'][i], W['ffn1_w_down'][i])
        h = _rmsnorm(x, W['g_mix'][i])
        j = i // N_MIXERS
        if i % N_MIXERS == 0:
            y, st = _pool_mixer(h, pool_hist[j], start_pos, W['pool_w'][j], W['pool_b'][j], W['pool_scale'][j])
            new_pool.append(st)
        else:
            y, st = _conv_module(h, conv_hist[j], W['conv_w_pw1'][j], W['conv_b_pw1'][j], W['conv_w_dw'][j],
                                 W['conv_b_dw'][j], W['conv_ln_g'][j], W['conv_ln_b'][j],
                                 W['conv_w_pw2'][j], W['conv_b_pw2'][j])
            new_conv.append(st)
        x = x + y
        h = _rmsnorm(x, W['g_ffn2'][i])
        x = x + 0.5 * _swiglu(h, W['ffn2_w_gate'][i], W['ffn2_w_up'][i], W['ffn2_w_down'][i])
        gate = jax.nn.sigmoid(_rmsnorm(x, W['g_ple'][i]) @ W['ple_w_gate'][i])
        x = x + (p[i] @ W['ple_w_proj'][i]) * gate
    return _rmsnorm(x, W['g_final']), jnp.stack(new_pool), jnp.stack(new_conv)


def setup_inputs(seed: int = 0) -> dict:
    key = jax.random.key(seed)
    ks = iter(jax.random.split(key, 40))
    nrm = lambda shape, s: jax.random.normal(next(ks), shape, jnp.float32) * s
    gain = lambda shape: 1.0 + 0.05 * jax.random.normal(next(ks), shape, jnp.float32)
    D, F, G = D_MODEL, D_FF, POOL_GROUP
    NP, NC = N_POOL_LAYERS, N_CONV_LAYERS
    return {
        'x_prompt': nrm((BATCH, SEQ, D), 1.0),
        'x_sample': nrm((DEC_BATCH, DEC_SEQ, D), 1.0),
        'state_pool': nrm((NP, DEC_BATCH, POOL_HIST, D), 1.0),
        'state_conv': nrm((NC, DEC_BATCH, CONV_HIST, D), 0.5),
        'p_prompt': nrm((DEPTH, BATCH, SEQ, PLE_DIM), 1.0),
        'p_sample': nrm((DEPTH, DEC_BATCH, DEC_SEQ, PLE_DIM), 1.0),
        'g_ffn1': gain((DEPTH, D)),
        'ffn1_w_gate': nrm((DEPTH, D, F), D ** -0.5),
        'ffn1_w_up': nrm((DEPTH, D, F), D ** -0.5),
        'ffn1_w_down': nrm((DEPTH, F, D), F ** -0.5),
        'g_mix': gain((DEPTH, D)),
        'pool_w': nrm((NP, N_POOL_GROUPS, G, G), G ** -0.5),
        'pool_b': nrm((NP, N_POOL_GROUPS, G), 0.02),
        'pool_scale': gain((NP, D)),
        'conv_w_pw1': nrm((NC, D, 2 * D), D ** -0.5),
        'conv_b_pw1': nrm((NC, 2 * D), 0.02),
        'conv_w_dw': nrm((NC, CONV_WIDTH, D), CONV_WIDTH ** -0.5),
        'conv_b_dw': nrm((NC, D), 0.02),
        'conv_ln_g': gain((NC, D)),
        'conv_ln_b': nrm((NC, D), 0.02),
        'conv_w_pw2': nrm((NC, D, D), D ** -0.5),
        'conv_b_pw2': nrm((NC, D), 0.02),
        'g_ffn2': gain((DEPTH, D)),
        'ffn2_w_gate': nrm((DEPTH, D, F), D ** -0.5),
        'ffn2_w_up': nrm((DEPTH, D, F), D ** -0.5),
        'ffn2_w_down': nrm((DEPTH, F, D), F ** -0.5),
        'g_ple': gain((DEPTH, D)),
        'ple_w_gate': nrm((DEPTH, D, D), D ** -0.5),
        'ple_w_proj': nrm((DEPTH, PLE_DIM, D), PLE_DIM ** -0.5),
        'g_final': gain((D,)),
    }


def reference(x_prompt, x_sample, state_pool, state_conv, p_prompt, p_sample,
              g_ffn1, ffn1_w_gate, ffn1_w_up, ffn1_w_down, g_mix,
              pool_w, pool_b, pool_scale,
              conv_w_pw1, conv_b_pw1, conv_w_dw, conv_b_dw, conv_ln_g, conv_ln_b, conv_w_pw2, conv_b_pw2,
              g_ffn2, ffn2_w_gate, ffn2_w_up, ffn2_w_down, g_ple, ple_w_gate, ple_w_proj, g_final):
    W = {
        'g_ffn1': g_ffn1, 'ffn1_w_gate': ffn1_w_gate, 'ffn1_w_up': ffn1_w_up, 'ffn1_w_down': ffn1_w_down,
        'g_mix': g_mix, 'pool_w': pool_w, 'pool_b': pool_b, 'pool_scale': pool_scale,
        'conv_w_pw1': conv_w_pw1, 'conv_b_pw1': conv_b_pw1, 'conv_w_dw': conv_w_dw, 'conv_b_dw': conv_b_dw,
        'conv_ln_g': conv_ln_g, 'conv_ln_b': conv_ln_b, 'conv_w_pw2': conv_w_pw2, 'conv_b_pw2': conv_b_pw2,
        'g_ffn2': g_ffn2, 'ffn2_w_gate': ffn2_w_gate, 'ffn2_w_up': ffn2_w_up, 'ffn2_w_down': ffn2_w_down,
        'g_ple': g_ple, 'ple_w_gate': ple_w_gate, 'ple_w_proj': ple_w_proj, 'g_final': g_final,
    }
    B = x_prompt.shape[0]
    zero_pool = jnp.zeros((N_POOL_LAYERS, B, POOL_HIST, D_MODEL), x_prompt.dtype)
    zero_conv = jnp.zeros((N_CONV_LAYERS, B, CONV_HIST, D_MODEL), x_prompt.dtype)
    y_prompt, new_pool_prompt, new_conv_prompt = _trunk(x_prompt, p_prompt, zero_pool, zero_conv, 0, W)
    y_sample, new_pool_sample, new_conv_sample = _trunk(x_sample, p_sample, state_pool, state_conv, PAST_LEN, W)
    return (y_prompt, y_sample, new_pool_prompt, new_conv_prompt, new_pool_sample, new_conv_sample)
```

```python
import functools

import jax
import jax.numpy as jnp
from jax import lax
from jax.experimental import pallas as pl
from jax.experimental.pallas import tpu as pltpu

EPS = 1e-6
POOL_WINDOWS = (2, 4, 8, 16)
POOL_HIST = max(POOL_WINDOWS) - 1
POOL_HALO = 16
CONV_HALO = 32
CONV_COL_CHUNK = 256
SAMPLE_START_POS = 1024

F32 = jnp.float32
BF16 = jnp.bfloat16

V7X_SCOPED_VMEM_CAP_BYTES = 60000 * 1024
COMPILER_TEMP_BYTES = 24 * 1024 * 1024


def _tile(n, pref, align):
    if n <= pref:
        return n
    t = (pref // align) * align
    while t >= align:
        if n % t == 0:
            return t
        t -= align
    return n


def _nbytes(shape, dtype):
    n = jnp.dtype(dtype).itemsize
    for s in shape:
        n *= s
    return n


def _params(semantics, pipelined_bytes, scratch_bytes=0):
    need = 2 * pipelined_bytes + scratch_bytes + COMPILER_TEMP_BYTES
    return pltpu.CompilerParams(
        dimension_semantics=semantics,
        vmem_limit_bytes=min(need, V7X_SCOPED_VMEM_CAP_BYTES))


def _rms(x):
    return x * lax.rsqrt(jnp.mean(x * x, axis=-1, keepdims=True) + EPS)


def _rms_body(x_ref, g_ref, o_ref):
    o_ref[...] = (_rms(x_ref[...]) * g_ref[...]).astype(o_ref.dtype)


def _rmsnorm(x, g, out_dtype):
    m, d = x.shape
    tr = _tile(m, 256, 8)
    blocks = _nbytes((tr, d), F32) + _nbytes((tr, d), out_dtype) + _nbytes((1, d), F32)
    return pl.pallas_call(
        _rms_body,
        grid=(m // tr,),
        in_specs=[pl.BlockSpec((tr, d), lambda i: (i, 0)),
                  pl.BlockSpec((1, d), lambda i: (0, 0))],
        out_specs=pl.BlockSpec((tr, d), lambda i: (i, 0)),
        out_shape=jax.ShapeDtypeStruct((m, d), out_dtype),
        compiler_params=_params(("parallel",), blocks),
        name="rmsnorm",
    )(x, g.reshape(1, d))


def _swiglu_body(h_ref, wa_ref, wb_ref, o_ref):
    h = h_ref[...]
    a = jnp.dot(h, wa_ref[...], preferred_element_type=F32)
    b = jnp.dot(h, wb_ref[...], preferred_element_type=F32)
    o_ref[...] = (jax.nn.silu(a) * b).astype(o_ref.dtype)


def _glu_body(h_ref, wa_ref, wb_ref, ba_ref, bb_ref, o_ref):
    h = h_ref[...]
    a = jnp.dot(h, wa_ref[...], preferred_element_type=F32) + ba_ref[...]
    b = jnp.dot(h, wb_ref[...], preferred_element_type=F32) + bb_ref[...]
    o_ref[...] = (a * jax.nn.sigmoid(b)).astype(o_ref.dtype)


def _swiglu_up(h, w_gate, w_up):
    m, d = h.shape
    f = w_gate.shape[1]
    tm = _tile(m, 1024, 16)
    tn = _tile(f, 256, 128)
    blocks = (_nbytes((tm, d), BF16) + 2 * _nbytes((d, tn), BF16) + _nbytes((tm, tn), BF16))
    return pl.pallas_call(
        _swiglu_body,
        grid=(m // tm, f // tn),
        in_specs=[pl.BlockSpec((tm, d), lambda i, j: (i, 0)),
                  pl.BlockSpec((d, tn), lambda i, j: (0, j)),
                  pl.BlockSpec((d, tn), lambda i, j: (0, j))],
        out_specs=pl.BlockSpec((tm, tn), lambda i, j: (i, j)),
        out_shape=jax.ShapeDtypeStruct((m, f), BF16),
        compiler_params=_params(("parallel", "parallel"), blocks),
        name="swiglu_up",
    )(h, w_gate, w_up)


def _glu(h, w, b):
    m, d = h.shape
    n = w.shape[1] // 2
    tm = _tile(m, 1024, 16)
    tn = _tile(n, 256, 128)
    nj = n // tn
    b2 = b.reshape(1, 2 * n)
    blocks = (_nbytes((tm, d), BF16) + 2 * _nbytes((d, tn), BF16) + _nbytes((tm, tn), F32)
              + 2 * _nbytes((1, tn), F32))
    return pl.pallas_call(
        _glu_body,
        grid=(m // tm, nj),
        in_specs=[pl.BlockSpec((tm, d), lambda i, j: (i, 0)),
                  pl.BlockSpec((d, tn), lambda i, j: (0, j)),
                  pl.BlockSpec((d, tn), lambda i, j: (0, j + nj)),
                  pl.BlockSpec((1, tn), lambda i, j: (0, j)),
                  pl.BlockSpec((1, tn), lambda i, j: (0, j + nj))],
        out_specs=pl.BlockSpec((tm, tn), lambda i, j: (i, j)),
        out_shape=jax.ShapeDtypeStruct((m, n), F32),
        compiler_params=_params(("parallel", "parallel"), blocks),
        name="conv_pw1_glu",
    )(h, w, w, b2, b2)


def _residual_mm_body(a_ref, w_ref, b_ref, x_ref, o_ref, *, scale):
    y = jnp.dot(a_ref[...], w_ref[...], preferred_element_type=F32) + b_ref[...]
    o_ref[...] = x_ref[...] + scale * y


def _residual_mm_nobias_body(a_ref, w_ref, x_ref, o_ref, *, scale):
    y = jnp.dot(a_ref[...], w_ref[...], preferred_element_type=F32)
    o_ref[...] = x_ref[...] + scale * y


def _residual_mm(a, w, x, *, scale, bias=None, tm_pref, tn_pref, name):
    m, k = a.shape
    n = w.shape[1]
    tm = _tile(m, tm_pref, 16)
    tn = _tile(n, tn_pref, 128)
    blocks = (_nbytes((tm, k), BF16) + _nbytes((k, tn), BF16) + 2 * _nbytes((tm, tn), F32)
              + _nbytes((1, tn), F32))
    a_spec = pl.BlockSpec((tm, k), lambda i, j: (i, 0))
    w_spec = pl.BlockSpec((k, tn), lambda i, j: (0, j))
    v_spec = pl.BlockSpec((1, tn), lambda i, j: (0, j))
    x_spec = pl.BlockSpec((tm, tn), lambda i, j: (i, j))
    if bias is None:
        body = functools.partial(_residual_mm_nobias_body, scale=scale)
        in_specs, args = [a_spec, w_spec, x_spec], (a, w, x)
    else:
        body = functools.partial(_residual_mm_body, scale=scale)
        in_specs, args = [a_spec, w_spec, v_spec, x_spec], (a, w, bias.reshape(1, n), x)
    return pl.pallas_call(
        body,
        grid=(m // tm, n // tn),
        in_specs=in_specs,
        out_specs=x_spec,
        out_shape=jax.ShapeDtypeStruct((m, n), F32),
        compiler_params=_params(("parallel", "parallel"), blocks),
        name=name,
    )(*args)


def _ple_body(h_ref, wg_ref, p_ref, wp_ref, x_ref, o_ref):
    gate = jax.nn.sigmoid(jnp.dot(h_ref[...], wg_ref[...], preferred_element_type=F32))
    proj = jnp.dot(p_ref[...], wp_ref[...], preferred_element_type=F32)
    o_ref[...] = x_ref[...] + proj * gate


def _ple(h, w_gate, p, w_proj, x):
    m, d = h.shape
    e = p.shape[1]
    n = w_gate.shape[1]
    tm = _tile(m, 1024, 16)
    tn = _tile(n, 512, 128)
    blocks = (_nbytes((tm, d), BF16) + _nbytes((d, tn), BF16) + _nbytes((tm, e), BF16)
              + _nbytes((e, tn), BF16) + 2 * _nbytes((tm, tn), F32))
    return pl.pallas_call(
        _ple_body,
        grid=(m // tm, n // tn),
        in_specs=[pl.BlockSpec((tm, d), lambda i, j: (i, 0)),
                  pl.BlockSpec((d, tn), lambda i, j: (0, j)),
                  pl.BlockSpec((tm, e), lambda i, j: (i, 0)),
                  pl.BlockSpec((e, tn), lambda i, j: (0, j)),
                  pl.BlockSpec((tm, tn), lambda i, j: (i, j))],
        out_specs=pl.BlockSpec((tm, tn), lambda i, j: (i, j)),
        out_shape=jax.ShapeDtypeStruct((m, n), F32),
        compiler_params=_params(("parallel", "parallel"), blocks),
        name="ple",
    )(h, w_gate, p, w_proj, x)


def _pool_body(x_ref, halo_ref, g_ref, w_ref, b_ref, sc_ref, o_ref, st_ref, ext_ref,
               *, t_blk, start_pos, halo_is_state):
    i = pl.program_id(1)
    grp = w_ref.shape[1]
    g = g_ref[...]
    ext_ref[pl.ds(POOL_HALO, t_blk), :] = _rms(x_ref[0]) * g
    if halo_is_state:
        ext_ref[pl.ds(0, POOL_HALO), :] = halo_ref[0]
    else:
        ext_ref[pl.ds(0, POOL_HALO), :] = jnp.where(i > 0, _rms(halo_ref[0]) * g, 0.0)
    pos = start_pos + i * t_blk + lax.broadcasted_iota(jnp.int32, (t_blk, 1), 0)
    for gi, win in enumerate(POOL_WINDOWS):
        cols = pl.ds(gi * grp, grp)
        h = ext_ref[pl.ds(POOL_HALO, t_blk), cols]
        s = h
        for k in range(1, win):
            s = s + ext_ref[pl.ds(POOL_HALO - k, t_blk), cols]
        cnt = jnp.minimum(pos + 1, win).astype(F32)
        d = (s / cnt - h).astype(BF16)
        y = jnp.dot(d, w_ref[gi], preferred_element_type=F32) + b_ref[gi]
        o_ref[0, :, cols] = x_ref[0, :, cols] + y * sc_ref[:, cols]

    @pl.when(i == pl.num_programs(1) - 1)
    def _():
        st_ref[0] = ext_ref[pl.ds(t_blk, POOL_HALO), :]


def _pool_mixer(x, halo, halo_is_state, start_pos, g, w, b, scale):
    bsz, n, d = x.shape
    grp = w.shape[1]
    t_blk = _tile(n, 256, POOL_HALO)
    halo_per_blk = t_blk // POOL_HALO
    if halo_is_state:
        halo_map = lambda bi, i: (bi, 0, 0)
    else:
        halo_map = lambda bi, i: (bi, jnp.maximum(i * halo_per_blk - 1, 0), 0)
    blocks = (2 * _nbytes((t_blk, d), F32) + 2 * _nbytes((POOL_HALO, d), F32)
              + _nbytes(w.shape, BF16) + 3 * _nbytes((1, d), F32))
    scratch = _nbytes((POOL_HALO + t_blk, d), F32)
    body = functools.partial(_pool_body, t_blk=t_blk, start_pos=start_pos,
                             halo_is_state=halo_is_state)
    return pl.pallas_call(
        body,
        grid=(bsz, n // t_blk),
        in_specs=[pl.BlockSpec((1, t_blk, d), lambda bi, i: (bi, i, 0)),
                  pl.BlockSpec((1, POOL_HALO, d), halo_map),
                  pl.BlockSpec((1, d), lambda bi, i: (0, 0)),
                  pl.BlockSpec(w.shape, lambda bi, i: (0, 0, 0)),
                  pl.BlockSpec((len(POOL_WINDOWS), 1, grp), lambda bi, i: (0, 0, 0)),
                  pl.BlockSpec((1, d), lambda bi, i: (0, 0))],
        out_specs=[pl.BlockSpec((1, t_blk, d), lambda bi, i: (bi, i, 0)),
                   pl.BlockSpec((1, POOL_HALO, d), lambda bi, i: (bi, 0, 0))],
        out_shape=[jax.ShapeDtypeStruct((bsz, n, d), F32),
                   jax.ShapeDtypeStruct((bsz, POOL_HALO, d), F32)],
        scratch_shapes=[pltpu.VMEM((POOL_HALO + t_blk, d), F32)],
        compiler_params=_params(("parallel", "arbitrary"), blocks, scratch),
        name="pool_mixer",
    )(x, halo, g.reshape(1, d), w, b.reshape(len(POOL_WINDOWS), 1, grp), scale.reshape(1, d))


def _dwconv_body(u_ref, halo_ref, w_ref, bdw_ref, lng_ref, lnb_ref, o_ref, ext_ref, c_ref,
                 *, t_blk, halo_is_state):
    i = pl.program_id(1)
    width = w_ref.shape[0]
    d = u_ref.shape[-1]
    ext_ref[pl.ds(CONV_HALO, t_blk), :] = u_ref[0]
    if halo_is_state:
        ext_ref[pl.ds(0, CONV_HALO), :] = halo_ref[0]
    else:
        ext_ref[pl.ds(0, CONV_HALO), :] = jnp.where(i > 0, halo_ref[0], 0.0)
    first = CONV_HALO - (width - 1)
    cw = min(CONV_COL_CHUNK, d)
    for c0 in range(0, d, cw):
        cols = pl.ds(c0, cw)
        acc = w_ref[pl.ds(0, 1), cols] * ext_ref[pl.ds(first, t_blk), cols]
        for k in range(1, width):
            acc = acc + w_ref[pl.ds(k, 1), cols] * ext_ref[pl.ds(first + k, t_blk), cols]
        c_ref[:, cols] = acc + bdw_ref[:, cols]
    c = c_ref[...]
    cc = c - jnp.mean(c, axis=-1, keepdims=True)
    var = jnp.mean(cc * cc, axis=-1, keepdims=True)
    y = cc * lax.rsqrt(var + EPS) * lng_ref[...] + lnb_ref[...]
    o_ref[0] = jax.nn.silu(y).astype(o_ref.dtype)


def _dwconv_ln_swish(u, halo, halo_is_state, w_dw, b_dw, ln_g, ln_b):
    bsz, n, d = u.shape
    width = w_dw.shape[0]
    assert width - 1 <= CONV_HALO
    t_blk = _tile(n, 64, CONV_HALO) if n >= CONV_HALO else n
    if halo_is_state:
        halo_map = lambda bi, i: (bi, 0, 0)
    else:
        halo_per_blk = t_blk // CONV_HALO
        halo_map = lambda bi, i: (bi, jnp.maximum(i * halo_per_blk - 1, 0), 0)
    blocks = (_nbytes((t_blk, d), F32) + _nbytes((CONV_HALO, d), F32) + _nbytes((t_blk, d), BF16)
              + _nbytes((width, d), F32) + 3 * _nbytes((1, d), F32))
    scratch = _nbytes((CONV_HALO + t_blk, d), F32) + _nbytes((t_blk, d), F32)
    body = functools.partial(_dwconv_body, t_blk=t_blk, halo_is_state=halo_is_state)
    vec = pl.BlockSpec((1, d), lambda bi, i: (0, 0))
    return pl.pallas_call(
        body,
        grid=(bsz, n // t_blk),
        in_specs=[pl.BlockSpec((1, t_blk, d), lambda bi, i: (bi, i, 0)),
                  pl.BlockSpec((1, CONV_HALO, d), halo_map),
                  pl.BlockSpec((width, d), lambda bi, i: (0, 0)),
                  vec, vec, vec],
        out_specs=pl.BlockSpec((1, t_blk, d), lambda bi, i: (bi, i, 0)),
        out_shape=jax.ShapeDtypeStruct((bsz, n, d), BF16),
        scratch_shapes=[pltpu.VMEM((CONV_HALO + t_blk, d), F32),
                        pltpu.VMEM((t_blk, d), F32)],
        compiler_params=_params(("parallel", "parallel"), blocks, scratch),
        name="dwconv_ln_swish",
    )(u, halo, w_dw, b_dw.reshape(1, d), ln_g.reshape(1, d), ln_b.reshape(1, d))


def _pad_rows_front(state, rows):
    return jnp.pad(state, ((0, 0), (rows - state.shape[1], 0), (0, 0)))


def _ffn(x, g, w_gate, w_up, w_down):
    h = _rmsnorm(x, g, BF16)
    a = _swiglu_up(h, w_gate, w_up)
    return _residual_mm(a, w_down, x, scale=0.5, tm_pref=512, tn_pref=512, name="ffn_down")


def _trunk(x, p, pool_state, conv_state, start_pos, W):
    bsz, n, d = x.shape
    m = bsz * n
    depth = W['g_ffn1'].shape[0]
    x = x.reshape(m, d)
    new_pool, new_conv = [], []
    for i in range(depth):
        x = _ffn(x, W['g_ffn1'][i], W['ffn1_w_gate'][i], W['ffn1_w_up'][i], W['ffn1_w_down'][i])
        j = i // 2
        if i % 2 == 0:
            x3 = x.reshape(bsz, n, d)
            if pool_state is None:
                halo, is_state = x3, False
            else:
                halo, is_state = _pad_rows_front(pool_state[j], POOL_HALO), True
            x3, st = _pool_mixer(x3, halo, is_state, start_pos, W['g_mix'][i], W['pool_w'][j],
                                 W['pool_b'][j], W['pool_scale'][j])
            x = x3.reshape(m, d)
            new_pool.append(st[:, POOL_HALO - POOL_HIST:])
        else:
            h = _rmsnorm(x, W['g_mix'][i], BF16)
            u = _glu(h, W['conv_w_pw1'][j], W['conv_b_pw1'][j]).reshape(bsz, n, d)
            hist = W['conv_w_dw'][j].shape[0] - 1
            if conv_state is None:
                halo, is_state = u, False
                new_conv.append(u[:, n - hist:])
            else:
                halo, is_state = _pad_rows_front(conv_state[j], CONV_HALO), True
                new_conv.append(jnp.concatenate([conv_state[j], u], axis=1)[:, -hist:])
            c = _dwconv_ln_swish(u, halo, is_state, W['conv_w_dw'][j], W['conv_b_dw'][j],
                                 W['conv_ln_g'][j], W['conv_ln_b'][j])
            x = _residual_mm(c.reshape(m, d), W['conv_w_pw2'][j], x, scale=1.0,
                             bias=W['conv_b_pw2'][j], tm_pref=1024, tn_pref=512, name="conv_pw2")
        x = _ffn(x, W['g_ffn2'][i], W['ffn2_w_gate'][i], W['ffn2_w_up'][i], W['ffn2_w_down'][i])
        h = _rmsnorm(x, W['g_ple'][i], BF16)
        x = _ple(h, W['ple_w_gate'][i], p[i].reshape(m, -1), W['ple_w_proj'][i], x)
    y = _rmsnorm(x, W['g_final'], F32).reshape(bsz, n, d)
    return y, jnp.stack(new_pool), jnp.stack(new_conv)


_MATMUL_WEIGHTS = ('ffn1_w_gate', 'ffn1_w_up', 'ffn1_w_down', 'pool_w', 'conv_w_pw1', 'conv_w_pw2',
                   'ffn2_w_gate', 'ffn2_w_up', 'ffn2_w_down', 'ple_w_gate', 'ple_w_proj')


def kernel(x_prompt, x_sample, state_pool, state_conv, p_prompt, p_sample, g_ffn1, ffn1_w_gate, ffn1_w_up, ffn1_w_down, g_mix, pool_w, pool_b, pool_scale, conv_w_pw1, conv_b_pw1, conv_w_dw, conv_b_dw, conv_ln_g, conv_ln_b, conv_w_pw2, conv_b_pw2, g_ffn2, ffn2_w_gate, ffn2_w_up, ffn2_w_down, g_ple, ple_w_gate, ple_w_proj, g_final):
    W = {
        'g_ffn1': g_ffn1, 'ffn1_w_gate': ffn1_w_gate, 'ffn1_w_up': ffn1_w_up, 'ffn1_w_down': ffn1_w_down,
        'g_mix': g_mix, 'pool_w': pool_w, 'pool_b': pool_b, 'pool_scale': pool_scale,
        'conv_w_pw1': conv_w_pw1, 'conv_b_pw1': conv_b_pw1, 'conv_w_dw': conv_w_dw, 'conv_b_dw': conv_b_dw,
        'conv_ln_g': conv_ln_g, 'conv_ln_b': conv_ln_b, 'conv_w_pw2': conv_w_pw2, 'conv_b_pw2': conv_b_pw2,
        'g_ffn2': g_ffn2, 'ffn2_w_gate': ffn2_w_gate, 'ffn2_w_up': ffn2_w_up, 'ffn2_w_down': ffn2_w_down,
        'g_ple': g_ple, 'ple_w_gate': ple_w_gate, 'ple_w_proj': ple_w_proj, 'g_final': g_final,
    }
    for name in _MATMUL_WEIGHTS:
        W[name] = W[name].astype(BF16)
    y_prompt, pool_prompt, conv_prompt = _trunk(
        x_prompt, p_prompt.astype(BF16), None, None, 0, W)
    y_sample, pool_sample, conv_sample = _trunk(
        x_sample, p_sample.astype(BF16), state_pool, state_conv, SAMPLE_START_POS, W)
    return (y_prompt, y_sample, pool_prompt, conv_prompt, pool_sample, conv_sample)
```

```python
import functools

import jax
import jax.numpy as jnp
from jax import lax
from jax.experimental import pallas as pl
from jax.experimental.pallas import tpu as pltpu

EPS = 1e-6
POOL_WINDOWS = (2, 4, 8, 16)
POOL_HIST = max(POOL_WINDOWS) - 1
POOL_HALO = 16
CONV_HALO = 32
CONV_COL_CHUNK = 128
SUBLANES = 8
SAMPLE_START_POS = 1024

F32 = jnp.float32
BF16 = jnp.bfloat16

V7X_SCOPED_VMEM_CAP_BYTES = 60000 * 1024
COMPILER_TEMP_BYTES = 24 * 1024 * 1024


def _tile(n, pref, align):
    if n <= pref:
        return n
    t = (pref // align) * align
    while t >= align:
        if n % t == 0:
            return t
        t -= align
    return n


def _nbytes(shape, dtype):
    n = jnp.dtype(dtype).itemsize
    for s in shape:
        n *= s
    return n


def _params(semantics, pipelined_bytes, scratch_bytes=0):
    need = 2 * pipelined_bytes + scratch_bytes + COMPILER_TEMP_BYTES
    return pltpu.CompilerParams(
        dimension_semantics=semantics,
        vmem_limit_bytes=min(need, V7X_SCOPED_VMEM_CAP_BYTES))


def _rms(x):
    return x * lax.rsqrt(jnp.mean(x * x, axis=-1, keepdims=True) + EPS)


def _rms_body(x_ref, g_ref, o_ref):
    o_ref[...] = (_rms(x_ref[...]) * g_ref[...]).astype(o_ref.dtype)


def _rmsnorm(x, g, out_dtype):
    m, d = x.shape
    tr = _tile(m, 256, 8)
    blocks = _nbytes((tr, d), F32) + _nbytes((tr, d), out_dtype) + _nbytes((1, d), F32)
    return pl.pallas_call(
        _rms_body,
        grid=(m // tr,),
        in_specs=[pl.BlockSpec((tr, d), lambda i: (i, 0)),
                  pl.BlockSpec((1, d), lambda i: (0, 0))],
        out_specs=pl.BlockSpec((tr, d), lambda i: (i, 0)),
        out_shape=jax.ShapeDtypeStruct((m, d), out_dtype),
        compiler_params=_params(("parallel",), blocks),
        name="rmsnorm",
    )(x, g.reshape(1, d))


def _mxu_dot(lhs, w_ref):
    return jnp.dot(lhs, w_ref[...].astype(BF16), preferred_element_type=F32)


def _swiglu_body(h_ref, wa_ref, wb_ref, o_ref):
    h = h_ref[...]
    a = _mxu_dot(h, wa_ref)
    b = _mxu_dot(h, wb_ref)
    o_ref[...] = (jax.nn.silu(a) * b).astype(o_ref.dtype)


def _glu_body(h_ref, wa_ref, wb_ref, ba_ref, bb_ref, o_ref):
    h = h_ref[...]
    a = _mxu_dot(h, wa_ref) + ba_ref[...]
    b = _mxu_dot(h, wb_ref) + bb_ref[...]
    o_ref[...] = (a * jax.nn.sigmoid(b)).astype(o_ref.dtype)


def _swiglu_up(h, w_gate, w_up, layer):
    m, d = h.shape
    f = w_gate.shape[2]
    tm = _tile(m, 1024, 16)
    tn = _tile(f, 256, 128)
    blocks = (_nbytes((tm, d), BF16) + 2 * _nbytes((d, tn), w_gate.dtype) + _nbytes((tm, tn), BF16))
    w_spec = pl.BlockSpec((None, d, tn), lambda i, j: (layer, 0, j))
    return pl.pallas_call(
        _swiglu_body,
        grid=(m // tm, f // tn),
        in_specs=[pl.BlockSpec((tm, d), lambda i, j: (i, 0)), w_spec, w_spec],
        out_specs=pl.BlockSpec((tm, tn), lambda i, j: (i, j)),
        out_shape=jax.ShapeDtypeStruct((m, f), BF16),
        compiler_params=_params(("parallel", "parallel"), blocks),
        name="swiglu_up",
    )(h, w_gate, w_up)


def _glu(h, w, b, layer):
    m, d = h.shape
    n = w.shape[2] // 2
    tm = _tile(m, 1024, 16)
    tn = _tile(n, 256, 128)
    nj = n // tn
    b2 = b.reshape(1, 2 * n)
    blocks = (_nbytes((tm, d), BF16) + 2 * _nbytes((d, tn), w.dtype) + _nbytes((tm, tn), F32)
              + 2 * _nbytes((1, tn), F32))
    return pl.pallas_call(
        _glu_body,
        grid=(m // tm, nj),
        in_specs=[pl.BlockSpec((tm, d), lambda i, j: (i, 0)),
                  pl.BlockSpec((None, d, tn), lambda i, j: (layer, 0, j)),
                  pl.BlockSpec((None, d, tn), lambda i, j: (layer, 0, j + nj)),
                  pl.BlockSpec((1, tn), lambda i, j: (0, j)),
                  pl.BlockSpec((1, tn), lambda i, j: (0, j + nj))],
        out_specs=pl.BlockSpec((tm, tn), lambda i, j: (i, j)),
        out_shape=jax.ShapeDtypeStruct((m, n), F32),
        compiler_params=_params(("parallel", "parallel"), blocks),
        name="conv_pw1_glu",
    )(h, w, w, b2, b2)


def _residual_mm_body(a_ref, w_ref, b_ref, x_ref, o_ref, *, scale):
    y = _mxu_dot(a_ref[...], w_ref) + b_ref[...]
    o_ref[...] = x_ref[...] + scale * y


def _residual_mm_nobias_body(a_ref, w_ref, x_ref, o_ref, *, scale):
    y = _mxu_dot(a_ref[...], w_ref)
    o_ref[...] = x_ref[...] + scale * y


def _residual_mm(a, w, layer, x, *, scale, bias=None, tm_pref, tn_pref, name):
    m, k = a.shape
    n = w.shape[2]
    tm = _tile(m, tm_pref, 16)
    tn = _tile(n, tn_pref, 128)
    blocks = (_nbytes((tm, k), BF16) + _nbytes((k, tn), w.dtype) + 2 * _nbytes((tm, tn), F32)
              + _nbytes((1, tn), F32))
    a_spec = pl.BlockSpec((tm, k), lambda i, j: (i, 0))
    w_spec = pl.BlockSpec((None, k, tn), lambda i, j: (layer, 0, j))
    v_spec = pl.BlockSpec((1, tn), lambda i, j: (0, j))
    x_spec = pl.BlockSpec((tm, tn), lambda i, j: (i, j))
    if bias is None:
        body = functools.partial(_residual_mm_nobias_body, scale=scale)
        in_specs, args = [a_spec, w_spec, x_spec], (a, w, x)
    else:
        body = functools.partial(_residual_mm_body, scale=scale)
        in_specs, args = [a_spec, w_spec, v_spec, x_spec], (a, w, bias.reshape(1, n), x)
    return pl.pallas_call(
        body,
        grid=(m // tm, n // tn),
        in_specs=in_specs,
        out_specs=x_spec,
        out_shape=jax.ShapeDtypeStruct((m, n), F32),
        compiler_params=_params(("parallel", "parallel"), blocks),
        name=name,
    )(*args)


def _ple_body(h_ref, wg_ref, p_ref, wp_ref, x_ref, o_ref):
    gate = jax.nn.sigmoid(_mxu_dot(h_ref[...], wg_ref))
    proj = _mxu_dot(p_ref[...], wp_ref)
    o_ref[...] = x_ref[...] + proj * gate


def _ple(h, w_gate, p, w_proj, layer, x):
    m, d = h.shape
    e = p.shape[1]
    n = w_gate.shape[2]
    tm = _tile(m, 1024, 16)
    tn = _tile(n, 512, 128)
    blocks = (_nbytes((tm, d), BF16) + _nbytes((d, tn), w_gate.dtype) + _nbytes((tm, e), BF16)
              + _nbytes((e, tn), w_proj.dtype) + 2 * _nbytes((tm, tn), F32))
    return pl.pallas_call(
        _ple_body,
        grid=(m // tm, n // tn),
        in_specs=[pl.BlockSpec((tm, d), lambda i, j: (i, 0)),
                  pl.BlockSpec((None, d, tn), lambda i, j: (layer, 0, j)),
                  pl.BlockSpec((tm, e), lambda i, j: (i, 0)),
                  pl.BlockSpec((None, e, tn), lambda i, j: (layer, 0, j)),
                  pl.BlockSpec((tm, tn), lambda i, j: (i, j))],
        out_specs=pl.BlockSpec((tm, tn), lambda i, j: (i, j)),
        out_shape=jax.ShapeDtypeStruct((m, n), F32),
        compiler_params=_params(("parallel", "parallel"), blocks),
        name="ple",
    )(h, w_gate, p, w_proj, x)


def _pool_body(x_ref, halo_ref, g_ref, w_ref, b_ref, sc_ref, o_ref, st_ref, ext_ref,
               *, t_blk, start_pos, halo_is_state):
    i = pl.program_id(1)
    grp = w_ref.shape[1]
    g = g_ref[...]
    ext_ref[pl.ds(POOL_HALO, t_blk), :] = _rms(x_ref[0]) * g
    if halo_is_state:
        ext_ref[pl.ds(0, POOL_HALO), :] = halo_ref[0]
    else:
        ext_ref[pl.ds(0, POOL_HALO), :] = jnp.where(i > 0, _rms(halo_ref[0]) * g, 0.0)
    pos = start_pos + i * t_blk + lax.broadcasted_iota(jnp.int32, (t_blk, 1), 0)
    for gi, win in enumerate(POOL_WINDOWS):
        cols = pl.ds(gi * grp, grp)
        h = ext_ref[pl.ds(POOL_HALO, t_blk), cols]
        s = h
        for k in range(1, win):
            s = s + ext_ref[pl.ds(POOL_HALO - k, t_blk), cols]
        cnt = jnp.minimum(pos + 1, win).astype(F32)
        d = (s / cnt - h).astype(BF16)
        y = jnp.dot(d, w_ref[gi], preferred_element_type=F32) + b_ref[gi]
        o_ref[0, :, cols] = x_ref[0, :, cols] + y * sc_ref[:, cols]

    @pl.when(i == pl.num_programs(1) - 1)
    def _():
        st_ref[0] = ext_ref[pl.ds(t_blk, POOL_HALO), :]


def _pool_mixer(x, halo, halo_is_state, start_pos, g, w, layer, b, scale):
    bsz, n, d = x.shape
    n_grp, grp = w.shape[1], w.shape[2]
    t_blk = _tile(n, 256, POOL_HALO)
    halo_per_blk = t_blk // POOL_HALO
    if halo_is_state:
        halo_map = lambda bi, i: (bi, 0, 0)
    else:
        halo_map = lambda bi, i: (bi, jnp.maximum(i * halo_per_blk - 1, 0), 0)
    blocks = (2 * _nbytes((t_blk, d), F32) + 2 * _nbytes((POOL_HALO, d), F32)
              + _nbytes(w.shape[1:], w.dtype) + 3 * _nbytes((1, d), F32))
    scratch = _nbytes((POOL_HALO + t_blk, d), F32)
    body = functools.partial(_pool_body, t_blk=t_blk, start_pos=start_pos,
                             halo_is_state=halo_is_state)
    return pl.pallas_call(
        body,
        grid=(bsz, n // t_blk),
        in_specs=[pl.BlockSpec((1, t_blk, d), lambda bi, i: (bi, i, 0)),
                  pl.BlockSpec((1, POOL_HALO, d), halo_map),
                  pl.BlockSpec((1, d), lambda bi, i: (0, 0)),
                  pl.BlockSpec((None, n_grp, grp, grp), lambda bi, i: (layer, 0, 0, 0)),
                  pl.BlockSpec((n_grp, 1, grp), lambda bi, i: (0, 0, 0)),
                  pl.BlockSpec((1, d), lambda bi, i: (0, 0))],
        out_specs=[pl.BlockSpec((1, t_blk, d), lambda bi, i: (bi, i, 0)),
                   pl.BlockSpec((1, POOL_HALO, d), lambda bi, i: (bi, 0, 0))],
        out_shape=[jax.ShapeDtypeStruct((bsz, n, d), F32),
                   jax.ShapeDtypeStruct((bsz, POOL_HALO, d), F32)],
        scratch_shapes=[pltpu.VMEM((POOL_HALO + t_blk, d), F32)],
        compiler_params=_params(("parallel", "arbitrary"), blocks, scratch),
        name="pool_mixer",
    )(x, halo, g.reshape(1, d), w, b.reshape(n_grp, 1, grp), scale.reshape(1, d))


def _dwconv_body(u_ref, halo_ref, w_ref, bdw_ref, lng_ref, lnb_ref, o_ref, ext_ref, c_ref,
                 *, t_blk, halo_is_state):
    i = pl.program_id(1)
    width = w_ref.shape[0]
    d = u_ref.shape[-1]
    ext_ref[pl.ds(CONV_HALO, t_blk), :] = u_ref[0]
    if halo_is_state:
        ext_ref[pl.ds(0, CONV_HALO), :] = halo_ref[0]
    else:
        ext_ref[pl.ds(0, CONV_HALO), :] = jnp.where(i > 0, halo_ref[0], 0.0)
    first = CONV_HALO - (width - 1)
    cw = min(CONV_COL_CHUNK, d)
    for c0 in range(0, d, cw):
        cols = pl.ds(c0, cw)
        acc = None
        for r in range(SUBLANES):
            taps = [k for k in range(width) if (first + k) % SUBLANES == r]
            if not taps:
                continue
            rows = t_blk + (SUBLANES if r else 0)
            part = None
            for k in taps:
                term = w_ref[pl.ds(k, 1), cols] * ext_ref[pl.ds(first + k - r, rows), cols]
                part = term if part is None else part + term
            part = part[r:r + t_blk]
            acc = part if acc is None else acc + part
        c_ref[:, cols] = acc + bdw_ref[:, cols]
    c = c_ref[...]
    cc = c - jnp.mean(c, axis=-1, keepdims=True)
    var = jnp.mean(cc * cc, axis=-1, keepdims=True)
    y = cc * lax.rsqrt(var + EPS) * lng_ref[...] + lnb_ref[...]
    o_ref[0] = jax.nn.silu(y).astype(o_ref.dtype)


def _dwconv_ln_swish(u, halo, halo_is_state, w_dw, b_dw, ln_g, ln_b):
    bsz, n, d = u.shape
    width = w_dw.shape[0]
    assert width - 1 <= CONV_HALO
    t_blk = _tile(n, 128, CONV_HALO) if n >= CONV_HALO else n
    if halo_is_state:
        halo_map = lambda bi, i: (bi, 0, 0)
    else:
        halo_per_blk = t_blk // CONV_HALO
        halo_map = lambda bi, i: (bi, jnp.maximum(i * halo_per_blk - 1, 0), 0)
    blocks = (_nbytes((t_blk, d), F32) + _nbytes((CONV_HALO, d), F32) + _nbytes((t_blk, d), BF16)
              + _nbytes((width, d), F32) + 3 * _nbytes((1, d), F32))
    scratch = _nbytes((CONV_HALO + t_blk, d), F32) + _nbytes((t_blk, d), F32)
    body = functools.partial(_dwconv_body, t_blk=t_blk, halo_is_state=halo_is_state)
    vec = pl.BlockSpec((1, d), lambda bi, i: (0, 0))
    return pl.pallas_call(
        body,
        grid=(bsz, n // t_blk),
        in_specs=[pl.BlockSpec((1, t_blk, d), lambda bi, i: (bi, i, 0)),
                  pl.BlockSpec((1, CONV_HALO, d), halo_map),
                  pl.BlockSpec((width, d), lambda bi, i: (0, 0)),
                  vec, vec, vec],
        out_specs=pl.BlockSpec((1, t_blk, d), lambda bi, i: (bi, i, 0)),
        out_shape=jax.ShapeDtypeStruct((bsz, n, d), BF16),
        scratch_shapes=[pltpu.VMEM((CONV_HALO + t_blk, d), F32),
                        pltpu.VMEM((t_blk, d), F32)],
        compiler_params=_params(("parallel", "parallel"), blocks, scratch),
        name="dwconv_ln_swish",
    )(u, halo, w_dw, b_dw.reshape(1, d), ln_g.reshape(1, d), ln_b.reshape(1, d))


def _pad_rows_front(state, rows):
    return jnp.pad(state, ((0, 0), (rows - state.shape[1], 0), (0, 0)))


def _ffn(x, g, w_gate, w_up, w_down, layer):
    h = _rmsnorm(x, g, BF16)
    a = _swiglu_up(h, w_gate, w_up, layer)
    return _residual_mm(a, w_down, layer, x, scale=0.5, tm_pref=512, tn_pref=512, name="ffn_down")


def _trunk(x, p, pool_state, conv_state, start_pos, W):
    bsz, n, d = x.shape
    m = bsz * n
    depth = W['g_ffn1'].shape[0]
    x = x.reshape(m, d)
    new_pool, new_conv = [], []
    for i in range(depth):
        x = _ffn(x, W['g_ffn1'][i], W['ffn1_w_gate'], W['ffn1_w_up'], W['ffn1_w_down'], i)
        j = i // 2
        if i % 2 == 0:
            x3 = x.reshape(bsz, n, d)
            if pool_state is None:
                halo, is_state = x3, False
            else:
                halo, is_state = _pad_rows_front(pool_state[j], POOL_HALO), True
            x3, st = _pool_mixer(x3, halo, is_state, start_pos, W['g_mix'][i], W['pool_w'], j,
                                 W['pool_b'][j], W['pool_scale'][j])
            x = x3.reshape(m, d)
            new_pool.append(st[:, POOL_HALO - POOL_HIST:])
        else:
            h = _rmsnorm(x, W['g_mix'][i], BF16)
            u = _glu(h, W['conv_w_pw1'], W['conv_b_pw1'][j], j).reshape(bsz, n, d)
            hist = W['conv_w_dw'][j].shape[0] - 1
            if conv_state is None:
                halo, is_state = u, False
                new_conv.append(u[:, n - hist:])
            else:
                halo, is_state = _pad_rows_front(conv_state[j], CONV_HALO), True
                new_conv.append(jnp.concatenate([conv_state[j], u], axis=1)[:, -hist:])
            c = _dwconv_ln_swish(u, halo, is_state, W['conv_w_dw'][j], W['conv_b_dw'][j],
                                 W['conv_ln_g'][j], W['conv_ln_b'][j])
            x = _residual_mm(c.reshape(m, d), W['conv_w_pw2'], j, x, scale=1.0,
                             bias=W['conv_b_pw2'][j], tm_pref=1024, tn_pref=512, name="conv_pw2")
        x = _ffn(x, W['g_ffn2'][i], W['ffn2_w_gate'], W['ffn2_w_up'], W['ffn2_w_down'], i)
        h = _rmsnorm(x, W['g_ple'][i], BF16)
        x = _ple(h, W['ple_w_gate'], p[i].reshape(m, -1), W['ple_w_proj'], i, x)
    y = _rmsnorm(x, W['g_final'], F32).reshape(bsz, n, d)
    return y, jnp.stack(new_pool), jnp.stack(new_conv)


_PRECAST_WEIGHTS = ('ffn1_w_down', 'ffn2_w_down', 'pool_w')


def kernel(x_prompt, x_sample, state_pool, state_conv, p_prompt, p_sample, g_ffn1, ffn1_w_gate, ffn1_w_up, ffn1_w_down, g_mix, pool_w, pool_b, pool_scale, conv_w_pw1, conv_b_pw1, conv_w_dw, conv_b_dw, conv_ln_g, conv_ln_b, conv_w_pw2, conv_b_pw2, g_ffn2, ffn2_w_gate, ffn2_w_up, ffn2_w_down, g_ple, ple_w_gate, ple_w_proj, g_final):
    W = {
        'g_ffn1': g_ffn1, 'ffn1_w_gate': ffn1_w_gate, 'ffn1_w_up': ffn1_w_up, 'ffn1_w_down': ffn1_w_down,
        'g_mix': g_mix, 'pool_w': pool_w, 'pool_b': pool_b, 'pool_scale': pool_scale,
        'conv_w_pw1': conv_w_pw1, 'conv_b_pw1': conv_b_pw1, 'conv_w_dw': conv_w_dw, 'conv_b_dw': conv_b_dw,
        'conv_ln_g': conv_ln_g, 'conv_ln_b': conv_ln_b, 'conv_w_pw2': conv_w_pw2, 'conv_b_pw2': conv_b_pw2,
        'g_ffn2': g_ffn2, 'ffn2_w_gate': ffn2_w_gate, 'ffn2_w_up': ffn2_w_up, 'ffn2_w_down': ffn2_w_down,
        'g_ple': g_ple, 'ple_w_gate': ple_w_gate, 'ple_w_proj': ple_w_proj, 'g_final': g_final,
    }
    for name in _PRECAST_WEIGHTS:
        W[name] = W[name].astype(BF16)
    y_prompt, pool_prompt, conv_prompt = _trunk(
        x_prompt, p_prompt.astype(BF16), None, None, 0, W)
    y_sample, pool_sample, conv_sample = _trunk(
        x_sample, p_sample.astype(BF16), state_pool, state_conv, SAMPLE_START_POS, W)
    return (y_prompt, y_sample, pool_prompt, conv_prompt, pool_sample, conv_sample)
```

```python
import functools

import jax
import jax.numpy as jnp
from jax import lax
from jax.experimental import pallas as pl
from jax.experimental.pallas import tpu as pltpu

EPS = 1e-6
POOL_WINDOWS = (2, 4, 8, 16)
POOL_HIST = max(POOL_WINDOWS) - 1
POOL_HALO = 16
CONV_HALO = 32
CONV_COL_CHUNK = 128
SUBLANES = 8
LANES = 128
SAMPLE_START_POS = 1024

F32 = jnp.float32
BF16 = jnp.bfloat16

V7X_SCOPED_VMEM_CAP_BYTES = 60000 * 1024
COMPILER_TEMP_BYTES = 24 * 1024 * 1024


def _tile(n, pref, align):
    if n <= pref:
        return n
    t = (pref // align) * align
    while t >= align:
        if n % t == 0:
            return t
        t -= align
    return n


def _nbytes(shape, dtype):
    n = jnp.dtype(dtype).itemsize
    for s in shape:
        n *= s
    return n


def _params(semantics, pipelined_bytes, scratch_bytes=0):
    need = 2 * pipelined_bytes + scratch_bytes + COMPILER_TEMP_BYTES
    return pltpu.CompilerParams(
        dimension_semantics=semantics,
        vmem_limit_bytes=min(need, V7X_SCOPED_VMEM_CAP_BYTES))


def _rms(x):
    return x * lax.rsqrt(jnp.mean(x * x, axis=-1, keepdims=True) + EPS)


def _lane_partial_sumsq(x):
    sq = x * x
    part = sq[:, 0:LANES]
    for c in range(1, x.shape[1] // LANES):
        part = part + sq[:, c * LANES:(c + 1) * LANES]
    return part


def _row_rsqrt(ss_ref, d):
    return lax.rsqrt(jnp.sum(ss_ref[...], axis=-1, keepdims=True) / d + EPS)


def _mxu_dot(lhs, w_ref):
    return jnp.dot(lhs, w_ref[...].astype(BF16), preferred_element_type=F32)


def _emit_next(xn, gn_ref, xg_ref, ss_ref, j):
    xg_ref[...] = (xn * gn_ref[...]).astype(BF16)
    part = _lane_partial_sumsq(xn)

    @pl.when(j == 0)
    def _():
        ss_ref[...] = part

    @pl.when(j > 0)
    def _():
        ss_ref[...] += part


def _prep_body(x_ref, g_ref, xg_ref, ss_ref):
    x = x_ref[...]
    xg_ref[...] = (x * g_ref[...]).astype(BF16)
    ss_ref[...] = _lane_partial_sumsq(x)


def _prep(x, g):
    m, d = x.shape
    tr = _tile(m, 256, 16)
    blocks = _nbytes((tr, d), F32) + _nbytes((tr, d), BF16) + _nbytes((tr, LANES), F32)
    return pl.pallas_call(
        _prep_body,
        grid=(m // tr,),
        in_specs=[pl.BlockSpec((tr, d), lambda i: (i, 0)),
                  pl.BlockSpec((1, d), lambda i: (0, 0))],
        out_specs=[pl.BlockSpec((tr, d), lambda i: (i, 0)),
                   pl.BlockSpec((tr, LANES), lambda i: (i, 0))],
        out_shape=[jax.ShapeDtypeStruct((m, d), BF16),
                   jax.ShapeDtypeStruct((m, LANES), F32)],
        compiler_params=_params(("parallel",), blocks),
        name="rms_prep",
    )(x, g.reshape(1, d))


def _rms_body(x_ref, g_ref, o_ref):
    o_ref[...] = _rms(x_ref[...]) * g_ref[...]


def _rmsnorm_rows(x, g, row0, rows):
    d = x.shape[1]
    tr = _tile(rows, 256, SUBLANES)
    assert row0 % tr == 0
    blk0 = row0 // tr
    blocks = 2 * _nbytes((tr, d), F32)
    return pl.pallas_call(
        _rms_body,
        grid=(rows // tr,),
        in_specs=[pl.BlockSpec((tr, d), lambda i: (blk0 + i, 0)),
                  pl.BlockSpec((1, d), lambda i: (0, 0))],
        out_specs=pl.BlockSpec((tr, d), lambda i: (i, 0)),
        out_shape=jax.ShapeDtypeStruct((rows, d), F32),
        compiler_params=_params(("parallel",), blocks),
        name="rmsnorm",
    )(x, g.reshape(1, d))


def _swiglu_body(xg_ref, ss_ref, wa_ref, wb_ref, o_ref):
    xg = xg_ref[...]
    r = _row_rsqrt(ss_ref, xg.shape[1])
    a = _mxu_dot(xg, wa_ref) * r
    b = _mxu_dot(xg, wb_ref) * r
    o_ref[...] = (jax.nn.silu(a) * b).astype(o_ref.dtype)


def _glu_body(xg_ref, ss_ref, wa_ref, wb_ref, ba_ref, bb_ref, o_ref):
    xg = xg_ref[...]
    r = _row_rsqrt(ss_ref, xg.shape[1])
    a = _mxu_dot(xg, wa_ref) * r + ba_ref[...]
    b = _mxu_dot(xg, wb_ref) * r + bb_ref[...]
    o_ref[...] = (a * jax.nn.sigmoid(b)).astype(o_ref.dtype)


def _swiglu_up(xg, ss, w_gate, w_up, layer):
    m, d = xg.shape
    f = w_gate.shape[2]
    tm = _tile(m, 1040, 16)
    tn = _tile(f, 256, LANES)
    blocks = (_nbytes((tm, d), BF16) + _nbytes((tm, LANES), F32) + 2 * _nbytes((d, tn), w_gate.dtype)
              + _nbytes((tm, tn), BF16))
    w_spec = pl.BlockSpec((None, d, tn), lambda i, j: (layer, 0, j))
    return pl.pallas_call(
        _swiglu_body,
        grid=(m // tm, f // tn),
        in_specs=[pl.BlockSpec((tm, d), lambda i, j: (i, 0)),
                  pl.BlockSpec((tm, LANES), lambda i, j: (i, 0)), w_spec, w_spec],
        out_specs=pl.BlockSpec((tm, tn), lambda i, j: (i, j)),
        out_shape=jax.ShapeDtypeStruct((m, f), BF16),
        compiler_params=_params(("parallel", "parallel"), blocks),
        name="swiglu_up",
    )(xg, ss, w_gate, w_up)


def _glu(xg, ss, w, b, layer):
    m, d = xg.shape
    n = w.shape[2] // 2
    tm = _tile(m, 1040, 16)
    tn = _tile(n, 256, LANES)
    nj = n // tn
    b2 = b.reshape(1, 2 * n)
    blocks = (_nbytes((tm, d), BF16) + _nbytes((tm, LANES), F32) + 2 * _nbytes((d, tn), w.dtype)
              + _nbytes((tm, tn), F32) + 2 * _nbytes((1, tn), F32))
    return pl.pallas_call(
        _glu_body,
        grid=(m // tm, nj),
        in_specs=[pl.BlockSpec((tm, d), lambda i, j: (i, 0)),
                  pl.BlockSpec((tm, LANES), lambda i, j: (i, 0)),
                  pl.BlockSpec((None, d, tn), lambda i, j: (layer, 0, j)),
                  pl.BlockSpec((None, d, tn), lambda i, j: (layer, 0, j + nj)),
                  pl.BlockSpec((1, tn), lambda i, j: (0, j)),
                  pl.BlockSpec((1, tn), lambda i, j: (0, j + nj))],
        out_specs=pl.BlockSpec((tm, tn), lambda i, j: (i, j)),
        out_shape=jax.ShapeDtypeStruct((m, n), F32),
        compiler_params=_params(("parallel", "parallel"), blocks),
        name="conv_pw1_glu",
    )(xg, ss, w, w, b2, b2)


def _residual_mm_body(*refs, scale, has_bias, emit_next):
    a_ref, w_ref = refs[0], refs[1]
    pos = 2
    y = _mxu_dot(a_ref[...], w_ref)
    if has_bias:
        y = y + refs[pos][...]
        pos += 1
    xn = refs[pos][...] + scale * y
    pos += 1
    if emit_next:
        gn_ref, o_ref, xg_ref, ss_ref = refs[pos:pos + 4]
        o_ref[...] = xn
        _emit_next(xn, gn_ref, xg_ref, ss_ref, pl.program_id(1))
    else:
        refs[pos][...] = xn


def _residual_mm(a, w, layer, x, *, scale, bias=None, g_next=None, tm_pref, tn_pref, name):
    m, k = a.shape
    n = w.shape[2]
    tm = _tile(m, tm_pref, 16)
    tn = _tile(n, tn_pref, LANES)
    emit_next = g_next is not None
    blocks = (_nbytes((tm, k), BF16) + _nbytes((k, tn), w.dtype) + 2 * _nbytes((tm, tn), F32)
              + 2 * _nbytes((1, tn), F32) + _nbytes((tm, tn), BF16) + _nbytes((tm, LANES), F32))
    v_spec = pl.BlockSpec((1, tn), lambda i, j: (0, j))
    x_spec = pl.BlockSpec((tm, tn), lambda i, j: (i, j))
    in_specs = [pl.BlockSpec((tm, k), lambda i, j: (i, 0)),
                pl.BlockSpec((None, k, tn), lambda i, j: (layer, 0, j))]
    args = [a, w]
    if bias is not None:
        in_specs.append(v_spec)
        args.append(bias.reshape(1, n))
    in_specs.append(x_spec)
    args.append(x)
    out_specs, out_shape = x_spec, jax.ShapeDtypeStruct((m, n), F32)
    if emit_next:
        in_specs.append(v_spec)
        args.append(g_next.reshape(1, n))
        out_specs = [x_spec, x_spec, pl.BlockSpec((tm, LANES), lambda i, j: (i, 0))]
        out_shape = [out_shape, jax.ShapeDtypeStruct((m, n), BF16),
                     jax.ShapeDtypeStruct((m, LANES), F32)]
    body = functools.partial(_residual_mm_body, scale=scale, has_bias=bias is not None,
                             emit_next=emit_next)
    return pl.pallas_call(
        body,
        grid=(m // tm, n // tn),
        in_specs=in_specs,
        out_specs=out_specs,
        out_shape=out_shape,
        compiler_params=_params(("parallel", "arbitrary"), blocks),
        name=name,
    )(*args)


def _ple_body(*refs, emit_next):
    xg_ref, ss_ref, wg_ref, p_ref, wp_ref, x_ref = refs[:6]
    xg = xg_ref[...]
    gate = jax.nn.sigmoid(_mxu_dot(xg, wg_ref) * _row_rsqrt(ss_ref, xg.shape[1]))
    proj = _mxu_dot(p_ref[...], wp_ref)
    xn = x_ref[...] + proj * gate
    if emit_next:
        gn_ref, o_ref, xg_out_ref, ss_out_ref = refs[6:10]
        o_ref[...] = xn
        _emit_next(xn, gn_ref, xg_out_ref, ss_out_ref, pl.program_id(1))
    else:
        refs[6][...] = xn


def _ple(xg, ss, w_gate, p, w_proj, layer, x, g_next=None):
    m, d = xg.shape
    e = p.shape[1]
    n = w_gate.shape[2]
    tm = _tile(m, 1040, 16)
    tn = _tile(n, 512, LANES)
    emit_next = g_next is not None
    blocks = (_nbytes((tm, d), BF16) + 2 * _nbytes((tm, LANES), F32) + _nbytes((d, tn), w_gate.dtype)
              + _nbytes((tm, e), BF16) + _nbytes((e, tn), w_proj.dtype) + 2 * _nbytes((tm, tn), F32)
              + _nbytes((tm, tn), BF16) + _nbytes((1, tn), F32))
    x_spec = pl.BlockSpec((tm, tn), lambda i, j: (i, j))
    ss_spec = pl.BlockSpec((tm, LANES), lambda i, j: (i, 0))
    in_specs = [pl.BlockSpec((tm, d), lambda i, j: (i, 0)), ss_spec,
                pl.BlockSpec((None, d, tn), lambda i, j: (layer, 0, j)),
                pl.BlockSpec((tm, e), lambda i, j: (i, 0)),
                pl.BlockSpec((None, e, tn), lambda i, j: (layer, 0, j)),
                x_spec]
    args = [xg, ss, w_gate, p, w_proj, x]
    out_specs, out_shape = x_spec, jax.ShapeDtypeStruct((m, n), F32)
    if emit_next:
        in_specs.append(pl.BlockSpec((1, tn), lambda i, j: (0, j)))
        args.append(g_next.reshape(1, n))
        out_specs = [x_spec, x_spec, ss_spec]
        out_shape = [out_shape, jax.ShapeDtypeStruct((m, n), BF16),
                     jax.ShapeDtypeStruct((m, LANES), F32)]
    return pl.pallas_call(
        functools.partial(_ple_body, emit_next=emit_next),
        grid=(m // tm, n // tn),
        in_specs=in_specs,
        out_specs=out_specs,
        out_shape=out_shape,
        compiler_params=_params(("parallel", "arbitrary"), blocks),
        name="ple",
    )(*args)


class _Streams:
    def __init__(self, row0, n_seq, seq_len, t_blk, halo):
        assert row0 % t_blk == 0 and seq_len % t_blk == 0
        self.row0, self.n_seq, self.seq_len, self.t_blk, self.halo = row0, n_seq, seq_len, t_blk, halo
        self.blocks_per_seq = seq_len // t_blk

    def row_block(self, s, i):
        return (self.row0 // self.t_blk + s * self.blocks_per_seq + i, 0)

    def prev_rows_block(self, s, i):
        per_blk = self.t_blk // self.halo
        return (self.row0 // self.halo + s * (self.seq_len // self.halo)
                + jnp.maximum(i * per_blk - 1, 0), 0)


def _share_outputs(in_specs, args, prev_outs, n_shared):
    if prev_outs is None:
        return in_specs, args, {}
    aliases = {len(args) + k: k for k in range(n_shared)}
    return (in_specs + [pl.BlockSpec(memory_space=pl.ANY)] * n_shared,
            args + list(prev_outs[:n_shared]), aliases)


def _pool_body(x_ref, halo_ref, g_ref, w_ref, b_ref, sc_ref, gn_ref, *rest,
               t_blk, start_pos, halo_is_state, n_prev):
    o_ref, xg_ref, ss_ref, st_ref, ext_ref = rest[n_prev:]
    i = pl.program_id(1)
    grp = w_ref.shape[1]
    g = g_ref[...]
    ext_ref[pl.ds(POOL_HALO, t_blk), :] = _rms(x_ref[...]) * g
    if halo_is_state:
        ext_ref[pl.ds(0, POOL_HALO), :] = halo_ref[0]
    else:
        ext_ref[pl.ds(0, POOL_HALO), :] = jnp.where(i > 0, _rms(halo_ref[...]) * g, 0.0)
    pos = start_pos + i * t_blk + lax.broadcasted_iota(jnp.int32, (t_blk, 1), 0)
    for gi, win in enumerate(POOL_WINDOWS):
        cols = pl.ds(gi * grp, grp)
        h = ext_ref[pl.ds(POOL_HALO, t_blk), cols]
        s = h
        for k in range(1, win):
            s = s + ext_ref[pl.ds(POOL_HALO - k, t_blk), cols]
        cnt = jnp.minimum(pos + 1, win).astype(F32)
        d = (s / cnt - h).astype(BF16)
        y = jnp.dot(d, w_ref[gi], preferred_element_type=F32) + b_ref[gi]
        o_ref[:, cols] = x_ref[:, cols] + y * sc_ref[:, cols]
    xn = o_ref[...]
    xg_ref[...] = (xn * gn_ref[...]).astype(BF16)
    ss_ref[...] = _lane_partial_sumsq(xn)

    @pl.when(i == pl.num_programs(1) - 1)
    def _():
        st_ref[0] = ext_ref[pl.ds(t_blk, POOL_HALO), :]


def _pool_mixer(x, streams, state, start_pos, g, w, layer, b, scale, g_next, prev_outs):
    m, d = x.shape
    n_grp, grp = w.shape[1], w.shape[2]
    t_blk = streams.t_blk
    if state is None:
        halo, halo_spec = x, pl.BlockSpec((POOL_HALO, d), streams.prev_rows_block)
    else:
        halo, halo_spec = state, pl.BlockSpec((1, POOL_HALO, d), lambda s, i: (s, 0, 0))
    blocks = (2 * _nbytes((t_blk, d), F32) + 2 * _nbytes((POOL_HALO, d), F32)
              + _nbytes(w.shape[1:], w.dtype) + 4 * _nbytes((1, d), F32)
              + _nbytes((t_blk, d), BF16) + _nbytes((t_blk, LANES), F32))
    scratch = _nbytes((POOL_HALO + t_blk, d), F32)
    n_prev = 0 if prev_outs is None else 3
    body = functools.partial(_pool_body, t_blk=t_blk, start_pos=start_pos,
                             halo_is_state=state is not None, n_prev=n_prev)
    vec = pl.BlockSpec((1, d), lambda s, i: (0, 0))
    row_spec = pl.BlockSpec((t_blk, d), streams.row_block)
    in_specs, args, aliases = _share_outputs(
        [row_spec, halo_spec, vec,
         pl.BlockSpec((None, n_grp, grp, grp), lambda s, i: (layer, 0, 0, 0)),
         pl.BlockSpec((n_grp, 1, grp), lambda s, i: (0, 0, 0)), vec, vec],
        [x, halo, g.reshape(1, d), w, b.reshape(n_grp, 1, grp), scale.reshape(1, d),
         g_next.reshape(1, d)],
        prev_outs, n_prev)
    return pl.pallas_call(
        body,
        grid=(streams.n_seq, streams.blocks_per_seq),
        in_specs=in_specs,
        out_specs=[row_spec, row_spec, pl.BlockSpec((t_blk, LANES), streams.row_block),
                   pl.BlockSpec((1, POOL_HALO, d), lambda s, i: (s, 0, 0))],
        out_shape=[jax.ShapeDtypeStruct((m, d), F32), jax.ShapeDtypeStruct((m, d), BF16),
                   jax.ShapeDtypeStruct((m, LANES), F32),
                   jax.ShapeDtypeStruct((streams.n_seq, POOL_HALO, d), F32)],
        scratch_shapes=[pltpu.VMEM((POOL_HALO + t_blk, d), F32)],
        input_output_aliases=aliases,
        compiler_params=_params(("parallel", "arbitrary"), blocks, scratch),
        name="pool_mixer",
    )(*args)


def _dwconv_body(u_ref, halo_ref, w_ref, bdw_ref, lng_ref, lnb_ref, *rest,
                 t_blk, halo_is_state, n_prev):
    o_ref, ext_ref, c_ref = rest[n_prev:]
    i = pl.program_id(1)
    width = w_ref.shape[0]
    d = u_ref.shape[-1]
    ext_ref[pl.ds(CONV_HALO, t_blk), :] = u_ref[...]
    if halo_is_state:
        ext_ref[pl.ds(0, CONV_HALO), :] = halo_ref[0]
    else:
        ext_ref[pl.ds(0, CONV_HALO), :] = jnp.where(i > 0, halo_ref[...], 0.0)
    first = CONV_HALO - (width - 1)
    cw = min(CONV_COL_CHUNK, d)
    for c0 in range(0, d, cw):
        cols = pl.ds(c0, cw)
        acc = None
        for r in range(SUBLANES):
            taps = [k for k in range(width) if (first + k) % SUBLANES == r]
            if not taps:
                continue
            rows = t_blk + (SUBLANES if r else 0)
            part = None
            for k in taps:
                term = w_ref[pl.ds(k, 1), cols] * ext_ref[pl.ds(first + k - r, rows), cols]
                part = term if part is None else part + term
            part = part[r:r + t_blk]
            acc = part if acc is None else acc + part
        c_ref[:, cols] = acc + bdw_ref[:, cols]
    c = c_ref[...]
    cc = c - jnp.mean(c, axis=-1, keepdims=True)
    var = jnp.mean(cc * cc, axis=-1, keepdims=True)
    y = cc * lax.rsqrt(var + EPS) * lng_ref[...] + lnb_ref[...]
    o_ref[...] = jax.nn.silu(y).astype(o_ref.dtype)


def _dwconv_ln_swish(u, streams, state, w_dw, b_dw, ln_g, ln_b, prev_outs):
    m, d = u.shape
    width = w_dw.shape[0]
    assert width - 1 <= CONV_HALO
    t_blk = streams.t_blk
    if state is None:
        halo, halo_spec = u, pl.BlockSpec((CONV_HALO, d), streams.prev_rows_block)
    else:
        halo, halo_spec = state, pl.BlockSpec((1, CONV_HALO, d), lambda s, i: (s, 0, 0))
    blocks = (_nbytes((t_blk, d), F32) + _nbytes((CONV_HALO, d), F32) + _nbytes((t_blk, d), BF16)
              + _nbytes((width, d), F32) + 3 * _nbytes((1, d), F32))
    scratch = _nbytes((CONV_HALO + t_blk, d), F32) + _nbytes((t_blk, d), F32)
    n_prev = 0 if prev_outs is None else 1
    body = functools.partial(_dwconv_body, t_blk=t_blk, halo_is_state=state is not None,
                             n_prev=n_prev)
    vec = pl.BlockSpec((1, d), lambda s, i: (0, 0))
    row_spec = pl.BlockSpec((t_blk, d), streams.row_block)
    in_specs, args, aliases = _share_outputs(
        [row_spec, halo_spec, pl.BlockSpec((width, d), lambda s, i: (0, 0)), vec, vec, vec],
        [u, halo, w_dw, b_dw.reshape(1, d), ln_g.reshape(1, d), ln_b.reshape(1, d)],
        prev_outs, n_prev)
    return pl.pallas_call(
        body,
        grid=(streams.n_seq, streams.blocks_per_seq),
        in_specs=in_specs,
        out_specs=[row_spec],
        out_shape=[jax.ShapeDtypeStruct((m, d), BF16)],
        scratch_shapes=[pltpu.VMEM((CONV_HALO + t_blk, d), F32),
                        pltpu.VMEM((t_blk, d), F32)],
        input_output_aliases=aliases,
        compiler_params=_params(("parallel", "parallel"), blocks, scratch),
        name="dwconv_ln_swish",
    )(*args)


def _pad_rows_front(state, rows):
    return jnp.pad(state, ((0, 0), (rows - state.shape[1], 0), (0, 0)))


def _ffn(xg, ss, x, w_gate, w_up, w_down, layer, g_next):
    a = _swiglu_up(xg, ss, w_gate, w_up, layer)
    return _residual_mm(a, w_down, layer, x, scale=0.5, g_next=g_next,
                        tm_pref=640, tn_pref=256, name="ffn_down")


def _trunk(x_prompt, x_sample, state_pool, state_conv, p_prompt, p_sample, W):
    bp, n_p, d = x_prompt.shape
    bs, n_s, _ = x_sample.shape
    mp, ms = bp * n_p, bs * n_s
    depth = W['g_ffn1'].shape[0]
    x = jnp.concatenate([x_prompt.reshape(mp, d), x_sample.reshape(ms, d)], axis=0)
    p = jnp.concatenate([p_prompt.reshape(depth, mp, -1), p_sample.reshape(depth, ms, -1)],
                        axis=1).astype(BF16)

    def groups(prompt_blk, halo):
        return (_Streams(0, bp, n_p, _tile(n_p, prompt_blk, halo), halo),
                _Streams(mp, bs, n_s, _tile(n_s, prompt_blk, halo) if n_s >= halo else n_s, halo))

    new_pool_p, new_pool_s, new_conv_p, new_conv_s = [], [], [], []
    xg, ss = _prep(x, W['g_ffn1'][0])
    for i in range(depth):
        j = i // 2
        if i % 2 == 0:
            x = _ffn(xg, ss, x, W['ffn1_w_gate'], W['ffn1_w_up'], W['ffn1_w_down'], i, None)
            sp, s_s = groups(256, POOL_HALO)
            pool_args = (W['g_mix'][i], W['pool_w'], j, W['pool_b'][j], W['pool_scale'][j],
                         W['g_ffn2'][i])
            outs = _pool_mixer(x, sp, None, 0, *pool_args, None)
            new_pool_p.append(outs[3][:, POOL_HALO - POOL_HIST:])
            outs = _pool_mixer(x, s_s, _pad_rows_front(state_pool[j], POOL_HALO), SAMPLE_START_POS,
                               *pool_args, outs)
            new_pool_s.append(outs[3][:, POOL_HALO - POOL_HIST:])
            x, xg, ss = outs[:3]
        else:
            x, xg, ss = _ffn(xg, ss, x, W['ffn1_w_gate'], W['ffn1_w_up'], W['ffn1_w_down'], i,
                             W['g_mix'][i])
            u = _glu(xg, ss, W['conv_w_pw1'], W['conv_b_pw1'][j], j)
            hist = W['conv_w_dw'].shape[1] - 1
            sp, s_s = groups(128, CONV_HALO)
            conv_args = (W['conv_w_dw'][j], W['conv_b_dw'][j], W['conv_ln_g'][j], W['conv_ln_b'][j])
            outs = _dwconv_ln_swish(u, sp, None, *conv_args, None)
            outs = _dwconv_ln_swish(u, s_s, _pad_rows_front(state_conv[j], CONV_HALO), *conv_args, outs)
            new_conv_p.append(u[:mp].reshape(bp, n_p, d)[:, n_p - hist:])
            new_conv_s.append(jnp.concatenate([state_conv[j], u[mp:].reshape(bs, n_s, d)],
                                              axis=1)[:, -hist:])
            x, xg, ss = _residual_mm(outs[0], W['conv_w_pw2'], j, x, scale=1.0,
                                     bias=W['conv_b_pw2'][j], g_next=W['g_ffn2'][i],
                                     tm_pref=1040, tn_pref=512, name="conv_pw2")
        x, xg, ss = _ffn(xg, ss, x, W['ffn2_w_gate'], W['ffn2_w_up'], W['ffn2_w_down'], i,
                         W['g_ple'][i])
        if i + 1 < depth:
            x, xg, ss = _ple(xg, ss, W['ple_w_gate'], p[i], W['ple_w_proj'], i, x, W['g_ffn1'][i + 1])
        else:
            x = _ple(xg, ss, W['ple_w_gate'], p[i], W['ple_w_proj'], i, x)
    y_prompt = _rmsnorm_rows(x, W['g_final'], 0, mp).reshape(bp, n_p, d)
    y_sample = _rmsnorm_rows(x, W['g_final'], mp, ms).reshape(bs, n_s, d)
    return (y_prompt, y_sample, jnp.stack(new_pool_p), jnp.stack(new_conv_p),
            jnp.stack(new_pool_s), jnp.stack(new_conv_s))


_PRECAST_WEIGHTS = ('ffn1_w_down', 'ffn2_w_down', 'pool_w')


def kernel(x_prompt, x_sample, state_pool, state_conv, p_prompt, p_sample, g_ffn1, ffn1_w_gate, ffn1_w_up, ffn1_w_down, g_mix, pool_w, pool_b, pool_scale, conv_w_pw1, conv_b_pw1, conv_w_dw, conv_b_dw, conv_ln_g, conv_ln_b, conv_w_pw2, conv_b_pw2, g_ffn2, ffn2_w_gate, ffn2_w_up, ffn2_w_down, g_ple, ple_w_gate, ple_w_proj, g_final):
    W = {
        'g_ffn1': g_ffn1, 'ffn1_w_gate': ffn1_w_gate, 'ffn1_w_up': ffn1_w_up, 'ffn1_w_down': ffn1_w_down,
        'g_mix': g_mix, 'pool_w': pool_w, 'pool_b': pool_b, 'pool_scale': pool_scale,
        'conv_w_pw1': conv_w_pw1, 'conv_b_pw1': conv_b_pw1, 'conv_w_dw': conv_w_dw, 'conv_b_dw': conv_b_dw,
        'conv_ln_g': conv_ln_g, 'conv_ln_b': conv_ln_b, 'conv_w_pw2': conv_w_pw2, 'conv_b_pw2': conv_b_pw2,
        'g_ffn2': g_ffn2, 'ffn2_w_gate': ffn2_w_gate, 'ffn2_w_up': ffn2_w_up, 'ffn2_w_down': ffn2_w_down,
        'g_ple': g_ple, 'ple_w_gate': ple_w_gate, 'ple_w_proj': ple_w_proj, 'g_final': g_final,
    }
    for name in _PRECAST_WEIGHTS:
        W[name] = W[name].astype(BF16)
    return _trunk(x_prompt, x_sample, state_pool, state_conv, p_prompt, p_sample, W)
```

```python
import functools

import jax
import jax.numpy as jnp
from jax import lax
from jax.experimental import pallas as pl
from jax.experimental.pallas import tpu as pltpu

EPS = 1e-6
POOL_WINDOWS = (2, 4, 8, 16)
POOL_HIST = max(POOL_WINDOWS) - 1
POOL_HALO = 16
CONV_HALO = 32
CONV_COL_CHUNK = 128
SUBLANES = 8
LANES = 128
SAMPLE_START_POS = 1024

F32 = jnp.float32
BF16 = jnp.bfloat16

V7X_SCOPED_VMEM_CAP_BYTES = 60000 * 1024
COMPILER_TEMP_BYTES = 24 * 1024 * 1024


def _tile(n, pref, align):
    if n <= pref:
        return n
    t = (pref // align) * align
    while t >= align:
        if n % t == 0:
            return t
        t -= align
    return n


def _nbytes(shape, dtype):
    n = jnp.dtype(dtype).itemsize
    for s in shape:
        n *= s
    return n


def _params(semantics, pipelined_bytes, scratch_bytes=0):
    need = 2 * pipelined_bytes + scratch_bytes + COMPILER_TEMP_BYTES
    return pltpu.CompilerParams(
        dimension_semantics=semantics,
        vmem_limit_bytes=min(need, V7X_SCOPED_VMEM_CAP_BYTES))


def _rms(x):
    return x * lax.rsqrt(jnp.mean(x * x, axis=-1, keepdims=True) + EPS)


def _lane_partial_sumsq(x):
    sq = x * x
    part = sq[:, 0:LANES]
    for c in range(1, x.shape[1] // LANES):
        part = part + sq[:, c * LANES:(c + 1) * LANES]
    return part


def _row_rsqrt(ss_ref, d):
    return lax.rsqrt(jnp.sum(ss_ref[...], axis=-1, keepdims=True) / d + EPS)


def _mxu_dot(lhs, w_ref):
    return jnp.dot(lhs, w_ref[...].astype(BF16), preferred_element_type=F32)


def _emit_next(xn, gn_ref, xg_ref, ss_ref, j):
    xg_ref[...] = (xn * gn_ref[...]).astype(BF16)
    part = _lane_partial_sumsq(xn)

    @pl.when(j == 0)
    def _():
        ss_ref[...] = part

    @pl.when(j > 0)
    def _():
        ss_ref[...] += part


def _prep_body(x_ref, g_ref, xg_ref, ss_ref):
    x = x_ref[...]
    xg_ref[...] = (x * g_ref[...]).astype(BF16)
    ss_ref[...] = _lane_partial_sumsq(x)


def _prep(x, g):
    m, d = x.shape
    tr = _tile(m, 256, 16)
    blocks = _nbytes((tr, d), F32) + _nbytes((tr, d), BF16) + _nbytes((tr, LANES), F32)
    return pl.pallas_call(
        _prep_body,
        grid=(m // tr,),
        in_specs=[pl.BlockSpec((tr, d), lambda i: (i, 0)),
                  pl.BlockSpec((1, d), lambda i: (0, 0))],
        out_specs=[pl.BlockSpec((tr, d), lambda i: (i, 0)),
                   pl.BlockSpec((tr, LANES), lambda i: (i, 0))],
        out_shape=[jax.ShapeDtypeStruct((m, d), BF16),
                   jax.ShapeDtypeStruct((m, LANES), F32)],
        compiler_params=_params(("parallel",), blocks),
        name="rms_prep",
    )(x, g.reshape(1, d))


def _rms_body(x_ref, g_ref, o_ref):
    o_ref[...] = _rms(x_ref[...]) * g_ref[...]


def _rmsnorm_rows(x, g, row0, rows):
    d = x.shape[1]
    tr = _tile(rows, 256, SUBLANES)
    assert row0 % tr == 0
    blk0 = row0 // tr
    blocks = 2 * _nbytes((tr, d), F32)
    return pl.pallas_call(
        _rms_body,
        grid=(rows // tr,),
        in_specs=[pl.BlockSpec((tr, d), lambda i: (blk0 + i, 0)),
                  pl.BlockSpec((1, d), lambda i: (0, 0))],
        out_specs=pl.BlockSpec((tr, d), lambda i: (i, 0)),
        out_shape=jax.ShapeDtypeStruct((rows, d), F32),
        compiler_params=_params(("parallel",), blocks),
        name="rmsnorm",
    )(x, g.reshape(1, d))


GATED_ROWS = 2080


def _resident_rows_spec(tm, d):
    return pl.BlockSpec((tm, d), lambda i, j: (i, 0), pipeline_mode=pl.Buffered(1))


def _swiglu_body(xg_ref, ss_ref, wa_ref, wb_ref, o_ref):
    xg = xg_ref[...]
    r = _row_rsqrt(ss_ref, xg.shape[1])
    a = _mxu_dot(xg, wa_ref) * r
    b = _mxu_dot(xg, wb_ref) * r
    o_ref[...] = (jax.nn.silu(a) * b).astype(o_ref.dtype)


def _glu_body(xg_ref, ss_ref, wa_ref, wb_ref, ba_ref, bb_ref, o_ref):
    xg = xg_ref[...]
    r = _row_rsqrt(ss_ref, xg.shape[1])
    a = _mxu_dot(xg, wa_ref) * r + ba_ref[...]
    b = _mxu_dot(xg, wb_ref) * r + bb_ref[...]
    o_ref[...] = (a * jax.nn.sigmoid(b)).astype(o_ref.dtype)


def _swiglu_up(xg, ss, w_gate, w_up, layer):
    m, d = xg.shape
    f = w_gate.shape[2]
    tm = _tile(m, GATED_ROWS, 16)
    tn = _tile(f, 256, LANES)
    blocks = (_nbytes((tm, LANES), F32) + 2 * _nbytes((d, tn), w_gate.dtype) + _nbytes((tm, tn), BF16))
    w_spec = pl.BlockSpec((None, d, tn), lambda i, j: (layer, 0, j))
    return pl.pallas_call(
        _swiglu_body,
        grid=(m // tm, f // tn),
        in_specs=[_resident_rows_spec(tm, d),
                  pl.BlockSpec((tm, LANES), lambda i, j: (i, 0)), w_spec, w_spec],
        out_specs=pl.BlockSpec((tm, tn), lambda i, j: (i, j)),
        out_shape=jax.ShapeDtypeStruct((m, f), BF16),
        compiler_params=_params(("parallel", "parallel"), blocks, _nbytes((tm, d), BF16)),
        name="swiglu_up",
    )(xg, ss, w_gate, w_up)


def _glu(xg, ss, w, b, layer):
    m, d = xg.shape
    n = w.shape[2] // 2
    tm = _tile(m, GATED_ROWS, 16)
    tn = _tile(n, 256, LANES)
    nj = n // tn
    b2 = b.reshape(1, 2 * n)
    blocks = (_nbytes((tm, LANES), F32) + 2 * _nbytes((d, tn), w.dtype)
              + _nbytes((tm, tn), F32) + 2 * _nbytes((1, tn), F32))
    return pl.pallas_call(
        _glu_body,
        grid=(m // tm, nj),
        in_specs=[_resident_rows_spec(tm, d),
                  pl.BlockSpec((tm, LANES), lambda i, j: (i, 0)),
                  pl.BlockSpec((None, d, tn), lambda i, j: (layer, 0, j)),
                  pl.BlockSpec((None, d, tn), lambda i, j: (layer, 0, j + nj)),
                  pl.BlockSpec((1, tn), lambda i, j: (0, j)),
                  pl.BlockSpec((1, tn), lambda i, j: (0, j + nj))],
        out_specs=pl.BlockSpec((tm, tn), lambda i, j: (i, j)),
        out_shape=jax.ShapeDtypeStruct((m, n), F32),
        compiler_params=_params(("parallel", "parallel"), blocks, _nbytes((tm, d), BF16)),
        name="conv_pw1_glu",
    )(xg, ss, w, w, b2, b2)


def _residual_mm_body(*refs, scale, has_bias, emit_next):
    a_ref, w_ref = refs[0], refs[1]
    pos = 2
    y = _mxu_dot(a_ref[...], w_ref)
    if has_bias:
        y = y + refs[pos][...]
        pos += 1
    xn = refs[pos][...] + scale * y
    pos += 1
    if emit_next:
        gn_ref, o_ref, xg_ref, ss_ref = refs[pos:pos + 4]
        o_ref[...] = xn
        _emit_next(xn, gn_ref, xg_ref, ss_ref, pl.program_id(1))
    else:
        refs[pos][...] = xn


def _residual_mm(a, w, layer, x, *, scale, bias=None, g_next=None, tm_pref, tn_pref, name):
    m, k = a.shape
    n = w.shape[2]
    tm = _tile(m, tm_pref, 16)
    tn = _tile(n, tn_pref, LANES)
    emit_next = g_next is not None
    blocks = (_nbytes((tm, k), BF16) + _nbytes((k, tn), w.dtype) + 2 * _nbytes((tm, tn), F32)
              + 2 * _nbytes((1, tn), F32) + _nbytes((tm, tn), BF16) + _nbytes((tm, LANES), F32))
    v_spec = pl.BlockSpec((1, tn), lambda i, j: (0, j))
    x_spec = pl.BlockSpec((tm, tn), lambda i, j: (i, j))
    in_specs = [pl.BlockSpec((tm, k), lambda i, j: (i, 0)),
                pl.BlockSpec((None, k, tn), lambda i, j: (layer, 0, j))]
    args = [a, w]
    if bias is not None:
        in_specs.append(v_spec)
        args.append(bias.reshape(1, n))
    in_specs.append(x_spec)
    args.append(x)
    out_specs, out_shape = x_spec, jax.ShapeDtypeStruct((m, n), F32)
    if emit_next:
        in_specs.append(v_spec)
        args.append(g_next.reshape(1, n))
        out_specs = [x_spec, x_spec, pl.BlockSpec((tm, LANES), lambda i, j: (i, 0))]
        out_shape = [out_shape, jax.ShapeDtypeStruct((m, n), BF16),
                     jax.ShapeDtypeStruct((m, LANES), F32)]
    body = functools.partial(_residual_mm_body, scale=scale, has_bias=bias is not None,
                             emit_next=emit_next)
    return pl.pallas_call(
        body,
        grid=(m // tm, n // tn),
        in_specs=in_specs,
        out_specs=out_specs,
        out_shape=out_shape,
        compiler_params=_params(("parallel", "arbitrary"), blocks),
        name=name,
    )(*args)


def _ple_body(*refs, emit_next):
    xg_ref, ss_ref, wg_ref, p_ref, wp_ref, x_ref = refs[:6]
    xg = xg_ref[...]
    gate = jax.nn.sigmoid(_mxu_dot(xg, wg_ref) * _row_rsqrt(ss_ref, xg.shape[1]))
    proj = _mxu_dot(p_ref[...], wp_ref)
    xn = x_ref[...] + proj * gate
    if emit_next:
        gn_ref, o_ref, xg_out_ref, ss_out_ref = refs[6:10]
        o_ref[...] = xn
        _emit_next(xn, gn_ref, xg_out_ref, ss_out_ref, pl.program_id(1))
    else:
        refs[6][...] = xn


def _ple(xg, ss, w_gate, p, w_proj, layer, x, g_next=None):
    m, d = xg.shape
    e = p.shape[1]
    n = w_gate.shape[2]
    tm = _tile(m, 1040, 16)
    tn = _tile(n, 512, LANES)
    emit_next = g_next is not None
    blocks = (_nbytes((tm, d), BF16) + 2 * _nbytes((tm, LANES), F32) + _nbytes((d, tn), w_gate.dtype)
              + _nbytes((tm, e), BF16) + _nbytes((e, tn), w_proj.dtype) + 2 * _nbytes((tm, tn), F32)
              + _nbytes((tm, tn), BF16) + _nbytes((1, tn), F32))
    x_spec = pl.BlockSpec((tm, tn), lambda i, j: (i, j))
    ss_spec = pl.BlockSpec((tm, LANES), lambda i, j: (i, 0))
    in_specs = [pl.BlockSpec((tm, d), lambda i, j: (i, 0)), ss_spec,
                pl.BlockSpec((None, d, tn), lambda i, j: (layer, 0, j)),
                pl.BlockSpec((tm, e), lambda i, j: (i, 0)),
                pl.BlockSpec((None, e, tn), lambda i, j: (layer, 0, j)),
                x_spec]
    args = [xg, ss, w_gate, p, w_proj, x]
    out_specs, out_shape = x_spec, jax.ShapeDtypeStruct((m, n), F32)
    if emit_next:
        in_specs.append(pl.BlockSpec((1, tn), lambda i, j: (0, j)))
        args.append(g_next.reshape(1, n))
        out_specs = [x_spec, x_spec, ss_spec]
        out_shape = [out_shape, jax.ShapeDtypeStruct((m, n), BF16),
                     jax.ShapeDtypeStruct((m, LANES), F32)]
    return pl.pallas_call(
        functools.partial(_ple_body, emit_next=emit_next),
        grid=(m // tm, n // tn),
        in_specs=in_specs,
        out_specs=out_specs,
        out_shape=out_shape,
        compiler_params=_params(("parallel", "arbitrary"), blocks),
        name="ple",
    )(*args)


class _Streams:
    def __init__(self, row0, n_seq, seq_len, t_blk, halo):
        assert row0 % t_blk == 0 and seq_len % t_blk == 0
        self.row0, self.n_seq, self.seq_len, self.t_blk, self.halo = row0, n_seq, seq_len, t_blk, halo
        self.blocks_per_seq = seq_len // t_blk

    def row_block(self, s, i):
        return (self.row0 // self.t_blk + s * self.blocks_per_seq + i, 0)

    def prev_rows_block(self, s, i):
        per_blk = self.t_blk // self.halo
        return (self.row0 // self.halo + s * (self.seq_len // self.halo)
                + jnp.maximum(i * per_blk - 1, 0), 0)


def _share_outputs(in_specs, args, prev_outs, n_shared):
    if prev_outs is None:
        return in_specs, args, {}
    aliases = {len(args) + k: k for k in range(n_shared)}
    return (in_specs + [pl.BlockSpec(memory_space=pl.ANY)] * n_shared,
            args + list(prev_outs[:n_shared]), aliases)


def _pool_body(x_ref, halo_ref, g_ref, w_ref, b_ref, sc_ref, gn_ref, *rest,
               t_blk, start_pos, halo_is_state, n_prev):
    o_ref, xg_ref, ss_ref, st_ref, ext_ref = rest[n_prev:]
    i = pl.program_id(1)
    grp = w_ref.shape[1]
    g = g_ref[...]
    ext_ref[pl.ds(POOL_HALO, t_blk), :] = _rms(x_ref[...]) * g
    if halo_is_state:
        ext_ref[pl.ds(0, POOL_HALO), :] = halo_ref[0]
    else:
        ext_ref[pl.ds(0, POOL_HALO), :] = jnp.where(i > 0, _rms(halo_ref[...]) * g, 0.0)
    pos = start_pos + i * t_blk + lax.broadcasted_iota(jnp.int32, (t_blk, 1), 0)
    for gi, win in enumerate(POOL_WINDOWS):
        cols = pl.ds(gi * grp, grp)
        h = ext_ref[pl.ds(POOL_HALO, t_blk), cols]
        s = h
        for k in range(1, win):
            s = s + ext_ref[pl.ds(POOL_HALO - k, t_blk), cols]
        cnt = jnp.minimum(pos + 1, win).astype(F32)
        d = (s / cnt - h).astype(BF16)
        y = jnp.dot(d, w_ref[gi], preferred_element_type=F32) + b_ref[gi]
        o_ref[:, cols] = x_ref[:, cols] + y * sc_ref[:, cols]
    xn = o_ref[...]
    xg_ref[...] = (xn * gn_ref[...]).astype(BF16)
    ss_ref[...] = _lane_partial_sumsq(xn)

    @pl.when(i == pl.num_programs(1) - 1)
    def _():
        st_ref[0] = ext_ref[pl.ds(t_blk, POOL_HALO), :]


def _pool_mixer(x, streams, state, start_pos, g, w, layer, b, scale, g_next, prev_outs):
    m, d = x.shape
    n_grp, grp = w.shape[1], w.shape[2]
    t_blk = streams.t_blk
    if state is None:
        halo, halo_spec = x, pl.BlockSpec((POOL_HALO, d), streams.prev_rows_block)
    else:
        halo, halo_spec = state, pl.BlockSpec((1, POOL_HALO, d), lambda s, i: (s, 0, 0))
    blocks = (2 * _nbytes((t_blk, d), F32) + 2 * _nbytes((POOL_HALO, d), F32)
              + _nbytes(w.shape[1:], w.dtype) + 4 * _nbytes((1, d), F32)
              + _nbytes((t_blk, d), BF16) + _nbytes((t_blk, LANES), F32))
    scratch = _nbytes((POOL_HALO + t_blk, d), F32)
    n_prev = 0 if prev_outs is None else 3
    body = functools.partial(_pool_body, t_blk=t_blk, start_pos=start_pos,
                             halo_is_state=state is not None, n_prev=n_prev)
    vec = pl.BlockSpec((1, d), lambda s, i: (0, 0))
    row_spec = pl.BlockSpec((t_blk, d), streams.row_block)
    in_specs, args, aliases = _share_outputs(
        [row_spec, halo_spec, vec,
         pl.BlockSpec((None, n_grp, grp, grp), lambda s, i: (layer, 0, 0, 0)),
         pl.BlockSpec((n_grp, 1, grp), lambda s, i: (0, 0, 0)), vec, vec],
        [x, halo, g.reshape(1, d), w, b.reshape(n_grp, 1, grp), scale.reshape(1, d),
         g_next.reshape(1, d)],
        prev_outs, n_prev)
    return pl.pallas_call(
        body,
        grid=(streams.n_seq, streams.blocks_per_seq),
        in_specs=in_specs,
        out_specs=[row_spec, row_spec, pl.BlockSpec((t_blk, LANES), streams.row_block),
                   pl.BlockSpec((1, POOL_HALO, d), lambda s, i: (s, 0, 0))],
        out_shape=[jax.ShapeDtypeStruct((m, d), F32), jax.ShapeDtypeStruct((m, d), BF16),
                   jax.ShapeDtypeStruct((m, LANES), F32),
                   jax.ShapeDtypeStruct((streams.n_seq, POOL_HALO, d), F32)],
        scratch_shapes=[pltpu.VMEM((POOL_HALO + t_blk, d), F32)],
        input_output_aliases=aliases,
        compiler_params=_params(("parallel", "arbitrary"), blocks, scratch),
        name="pool_mixer",
    )(*args)


def _dwconv_body(u_ref, halo_ref, w_ref, bdw_ref, lng_ref, lnb_ref, *rest,
                 t_blk, halo_is_state, n_prev):
    o_ref, ext_ref, c_ref = rest[n_prev:]
    i = pl.program_id(1)
    width = w_ref.shape[0]
    d = u_ref.shape[-1]
    ext_ref[pl.ds(CONV_HALO, t_blk), :] = u_ref[...]
    if halo_is_state:
        ext_ref[pl.ds(0, CONV_HALO), :] = halo_ref[0]
    else:
        ext_ref[pl.ds(0, CONV_HALO), :] = jnp.where(i > 0, halo_ref[...], 0.0)
    first = CONV_HALO - (width - 1)
    cw = min(CONV_COL_CHUNK, d)
    for c0 in range(0, d, cw):
        cols = pl.ds(c0, cw)
        acc = None
        for r in range(SUBLANES):
            taps = [k for k in range(width) if (first + k) % SUBLANES == r]
            if not taps:
                continue
            rows = t_blk + (SUBLANES if r else 0)
            part = None
            for k in taps:
                term = w_ref[pl.ds(k, 1), cols] * ext_ref[pl.ds(first + k - r, rows), cols]
                part = term if part is None else part + term
            part = part[r:r + t_blk]
            acc = part if acc is None else acc + part
        c_ref[:, cols] = acc + bdw_ref[:, cols]
    c = c_ref[...]
    cc = c - jnp.mean(c, axis=-1, keepdims=True)
    var = jnp.mean(cc * cc, axis=-1, keepdims=True)
    y = cc * lax.rsqrt(var + EPS) * lng_ref[...] + lnb_ref[...]
    o_ref[...] = jax.nn.silu(y).astype(o_ref.dtype)


def _dwconv_ln_swish(u, streams, state, w_dw, b_dw, ln_g, ln_b, prev_outs):
    m, d = u.shape
    width = w_dw.shape[0]
    assert width - 1 <= CONV_HALO
    t_blk = streams.t_blk
    if state is None:
        halo, halo_spec = u, pl.BlockSpec((CONV_HALO, d), streams.prev_rows_block)
    else:
        halo, halo_spec = state, pl.BlockSpec((1, CONV_HALO, d), lambda s, i: (s, 0, 0))
    blocks = (_nbytes((t_blk, d), F32) + _nbytes((CONV_HALO, d), F32) + _nbytes((t_blk, d), BF16)
              + _nbytes((width, d), F32) + 3 * _nbytes((1, d), F32))
    scratch = _nbytes((CONV_HALO + t_blk, d), F32) + _nbytes((t_blk, d), F32)
    n_prev = 0 if prev_outs is None else 1
    body = functools.partial(_dwconv_body, t_blk=t_blk, halo_is_state=state is not None,
                             n_prev=n_prev)
    vec = pl.BlockSpec((1, d), lambda s, i: (0, 0))
    row_spec = pl.BlockSpec((t_blk, d), streams.row_block)
    in_specs, args, aliases = _share_outputs(
        [row_spec, halo_spec, pl.BlockSpec((width, d), lambda s, i: (0, 0)), vec, vec, vec],
        [u, halo, w_dw, b_dw.reshape(1, d), ln_g.reshape(1, d), ln_b.reshape(1, d)],
        prev_outs, n_prev)
    return pl.pallas_call(
        body,
        grid=(streams.n_seq, streams.blocks_per_seq),
        in_specs=in_specs,
        out_specs=[row_spec],
        out_shape=[jax.ShapeDtypeStruct((m, d), BF16)],
        scratch_shapes=[pltpu.VMEM((CONV_HALO + t_blk, d), F32),
                        pltpu.VMEM((t_blk, d), F32)],
        input_output_aliases=aliases,
        compiler_params=_params(("parallel", "parallel"), blocks, scratch),
        name="dwconv_ln_swish",
    )(*args)


def _pad_rows_front(state, rows):
    return jnp.pad(state, ((0, 0), (rows - state.shape[1], 0), (0, 0)))


def _ffn(xg, ss, x, w_gate, w_up, w_down, layer, g_next):
    a = _swiglu_up(xg, ss, w_gate, w_up, layer)
    return _residual_mm(a, w_down, layer, x, scale=0.5, g_next=g_next,
                        tm_pref=640, tn_pref=256, name="ffn_down")


def _trunk(x_prompt, x_sample, state_pool, state_conv, p_prompt, p_sample, W):
    bp, n_p, d = x_prompt.shape
    bs, n_s, _ = x_sample.shape
    mp, ms = bp * n_p, bs * n_s
    depth = W['g_ffn1'].shape[0]
    x = jnp.concatenate([x_prompt.reshape(mp, d), x_sample.reshape(ms, d)], axis=0)
    p = jnp.concatenate([p_prompt.reshape(depth, mp, -1), p_sample.reshape(depth, ms, -1)],
                        axis=1).astype(BF16)

    def groups(prompt_blk, halo):
        return (_Streams(0, bp, n_p, _tile(n_p, prompt_blk, halo), halo),
                _Streams(mp, bs, n_s, _tile(n_s, prompt_blk, halo) if n_s >= halo else n_s, halo))

    new_pool_p, new_pool_s, new_conv_p, new_conv_s = [], [], [], []
    xg, ss = _prep(x, W['g_ffn1'][0])
    for i in range(depth):
        j = i // 2
        if i % 2 == 0:
            x = _ffn(xg, ss, x, W['ffn1_w_gate'], W['ffn1_w_up'], W['ffn1_w_down'], i, None)
            sp, s_s = groups(256, POOL_HALO)
            pool_args = (W['g_mix'][i], W['pool_w'], j, W['pool_b'][j], W['pool_scale'][j],
                         W['g_ffn2'][i])
            outs = _pool_mixer(x, sp, None, 0, *pool_args, None)
            new_pool_p.append(outs[3][:, POOL_HALO - POOL_HIST:])
            outs = _pool_mixer(x, s_s, _pad_rows_front(state_pool[j], POOL_HALO), SAMPLE_START_POS,
                               *pool_args, outs)
            new_pool_s.append(outs[3][:, POOL_HALO - POOL_HIST:])
            x, xg, ss = outs[:3]
        else:
            x, xg, ss = _ffn(xg, ss, x, W['ffn1_w_gate'], W['ffn1_w_up'], W['ffn1_w_down'], i,
                             W['g_mix'][i])
            u = _glu(xg, ss, W['conv_w_pw1'], W['conv_b_pw1'][j], j)
            hist = W['conv_w_dw'].shape[1] - 1
            sp, s_s = groups(128, CONV_HALO)
            conv_args = (W['conv_w_dw'][j], W['conv_b_dw'][j], W['conv_ln_g'][j], W['conv_ln_b'][j])
            outs = _dwconv_ln_swish(u, sp, None, *conv_args, None)
            outs = _dwconv_ln_swish(u, s_s, _pad_rows_front(state_conv[j], CONV_HALO), *conv_args, outs)
            new_conv_p.append(jnp.stack([u[(s + 1) * n_p - hist:(s + 1) * n_p] for s in range(bp)]))
            new_conv_s.append(jnp.concatenate([state_conv[j], u[mp:].reshape(bs, n_s, d)],
                                              axis=1)[:, -hist:])
            x, xg, ss = _residual_mm(outs[0], W['conv_w_pw2'], j, x, scale=1.0,
                                     bias=W['conv_b_pw2'][j], g_next=W['g_ffn2'][i],
                                     tm_pref=1040, tn_pref=512, name="conv_pw2")
        x, xg, ss = _ffn(xg, ss, x, W['ffn2_w_gate'], W['ffn2_w_up'], W['ffn2_w_down'], i,
                         W['g_ple'][i])
        if i + 1 < depth:
            x, xg, ss = _ple(xg, ss, W['ple_w_gate'], p[i], W['ple_w_proj'], i, x, W['g_ffn1'][i + 1])
        else:
            x = _ple(xg, ss, W['ple_w_gate'], p[i], W['ple_w_proj'], i, x)
    y_prompt = _rmsnorm_rows(x, W['g_final'], 0, mp).reshape(bp, n_p, d)
    y_sample = _rmsnorm_rows(x, W['g_final'], mp, ms).reshape(bs, n_s, d)
    return (y_prompt, y_sample, jnp.stack(new_pool_p), jnp.stack(new_conv_p),
            jnp.stack(new_pool_s), jnp.stack(new_conv_s))


_PRECAST_WEIGHTS = ('ffn1_w_down', 'ffn2_w_down', 'pool_w')


def kernel(x_prompt, x_sample, state_pool, state_conv, p_prompt, p_sample, g_ffn1, ffn1_w_gate, ffn1_w_up, ffn1_w_down, g_mix, pool_w, pool_b, pool_scale, conv_w_pw1, conv_b_pw1, conv_w_dw, conv_b_dw, conv_ln_g, conv_ln_b, conv_w_pw2, conv_b_pw2, g_ffn2, ffn2_w_gate, ffn2_w_up, ffn2_w_down, g_ple, ple_w_gate, ple_w_proj, g_final):
    W = {
        'g_ffn1': g_ffn1, 'ffn1_w_gate': ffn1_w_gate, 'ffn1_w_up': ffn1_w_up, 'ffn1_w_down': ffn1_w_down,
        'g_mix': g_mix, 'pool_w': pool_w, 'pool_b': pool_b, 'pool_scale': pool_scale,
        'conv_w_pw1': conv_w_pw1, 'conv_b_pw1': conv_b_pw1, 'conv_w_dw': conv_w_dw, 'conv_b_dw': conv_b_dw,
        'conv_ln_g': conv_ln_g, 'conv_ln_b': conv_ln_b, 'conv_w_pw2': conv_w_pw2, 'conv_b_pw2': conv_b_pw2,
        'g_ffn2': g_ffn2, 'ffn2_w_gate': ffn2_w_gate, 'ffn2_w_up': ffn2_w_up, 'ffn2_w_down': ffn2_w_down,
        'g_ple': g_ple, 'ple_w_gate': ple_w_gate, 'ple_w_proj': ple_w_proj, 'g_final': g_final,
    }
    for name in _PRECAST_WEIGHTS:
        W[name] = W[name].astype(BF16)
    return _trunk(x_prompt, x_sample, state_pool, state_conv, p_prompt, p_sample, W)
```

```python
import functools

import jax
import jax.numpy as jnp
from jax import lax
from jax.experimental import pallas as pl
from jax.experimental.pallas import tpu as pltpu

EPS = 1e-6
POOL_WINDOWS = (2, 4, 8, 16)
POOL_HIST = max(POOL_WINDOWS) - 1
POOL_HALO = 16
CONV_HALO = 32
CONV_COL_CHUNK = 128
SUBLANES = 8
LANES = 128
SAMPLE_START_POS = 1024

F32 = jnp.float32
BF16 = jnp.bfloat16

V7X_SCOPED_VMEM_CAP_BYTES = 60000 * 1024
COMPILER_TEMP_BYTES = 24 * 1024 * 1024


def _tile(n, pref, align):
    if n <= pref:
        return n
    t = (pref // align) * align
    while t >= align:
        if n % t == 0:
            return t
        t -= align
    return n


def _nbytes(shape, dtype):
    n = jnp.dtype(dtype).itemsize
    for s in shape:
        n *= s
    return n


def _params(semantics, pipelined_bytes, scratch_bytes=0):
    need = 2 * pipelined_bytes + scratch_bytes + COMPILER_TEMP_BYTES
    return pltpu.CompilerParams(
        dimension_semantics=semantics,
        vmem_limit_bytes=min(need, V7X_SCOPED_VMEM_CAP_BYTES))


def _rms(x):
    return x * lax.rsqrt(jnp.mean(x * x, axis=-1, keepdims=True) + EPS)


def _lane_partial_sumsq(x):
    sq = x * x
    part = sq[:, 0:LANES]
    for c in range(1, x.shape[1] // LANES):
        part = part + sq[:, c * LANES:(c + 1) * LANES]
    return part


def _row_rsqrt(ss_ref, d):
    return lax.rsqrt(jnp.sum(ss_ref[...], axis=-1, keepdims=True) / d + EPS)


def _mxu_dot(lhs, w_ref):
    return jnp.dot(lhs, w_ref[...].astype(BF16), preferred_element_type=F32)


def _emit_next(xn, gn_ref, xg_ref, ss_ref, j):
    xg_ref[...] = (xn * gn_ref[...]).astype(BF16)
    part = _lane_partial_sumsq(xn)

    @pl.when(j == 0)
    def _():
        ss_ref[...] = part

    @pl.when(j > 0)
    def _():
        ss_ref[...] += part


def _prep_body(x_ref, g_ref, xg_ref, ss_ref):
    x = x_ref[...]
    xg_ref[...] = (x * g_ref[...]).astype(BF16)
    ss_ref[...] = _lane_partial_sumsq(x)


def _prep(x, g):
    m, d = x.shape
    tr = _tile(m, 256, 16)
    blocks = _nbytes((tr, d), F32) + _nbytes((tr, d), BF16) + _nbytes((tr, LANES), F32)
    return pl.pallas_call(
        _prep_body,
        grid=(m // tr,),
        in_specs=[pl.BlockSpec((tr, d), lambda i: (i, 0)),
                  pl.BlockSpec((1, d), lambda i: (0, 0))],
        out_specs=[pl.BlockSpec((tr, d), lambda i: (i, 0)),
                   pl.BlockSpec((tr, LANES), lambda i: (i, 0))],
        out_shape=[jax.ShapeDtypeStruct((m, d), BF16),
                   jax.ShapeDtypeStruct((m, LANES), F32)],
        compiler_params=_params(("parallel",), blocks),
        name="rms_prep",
    )(x, g.reshape(1, d))


def _rms_body(x_ref, g_ref, o_ref):
    o_ref[...] = _rms(x_ref[...]) * g_ref[...]


def _rmsnorm_rows(x, g, row0, rows):
    d = x.shape[1]
    tr = _tile(rows, 256, SUBLANES)
    assert row0 % tr == 0
    blk0 = row0 // tr
    blocks = 2 * _nbytes((tr, d), F32)
    return pl.pallas_call(
        _rms_body,
        grid=(rows // tr,),
        in_specs=[pl.BlockSpec((tr, d), lambda i: (blk0 + i, 0)),
                  pl.BlockSpec((1, d), lambda i: (0, 0))],
        out_specs=pl.BlockSpec((tr, d), lambda i: (i, 0)),
        out_shape=jax.ShapeDtypeStruct((rows, d), F32),
        compiler_params=_params(("parallel",), blocks),
        name="rmsnorm",
    )(x, g.reshape(1, d))


GATED_ROWS = 2080


def _resident_rows_spec(tm, d):
    return pl.BlockSpec((tm, d), lambda i, j: (i, 0), pipeline_mode=pl.Buffered(1))


def _swiglu_body(xg_ref, ss_ref, wa_ref, wb_ref, o_ref):
    xg = xg_ref[...]
    r = _row_rsqrt(ss_ref, xg.shape[1])
    a = _mxu_dot(xg, wa_ref) * r
    b = _mxu_dot(xg, wb_ref) * r
    o_ref[...] = (jax.nn.silu(a) * b).astype(o_ref.dtype)


def _glu_body(xg_ref, ss_ref, wa_ref, wb_ref, ba_ref, bb_ref, o_ref):
    xg = xg_ref[...]
    r = _row_rsqrt(ss_ref, xg.shape[1])
    a = _mxu_dot(xg, wa_ref) * r + ba_ref[...]
    b = _mxu_dot(xg, wb_ref) * r + bb_ref[...]
    o_ref[...] = (a * jax.nn.sigmoid(b)).astype(o_ref.dtype)


def _swiglu_up(xg, ss, w_gate, w_up, layer):
    m, d = xg.shape
    f = w_gate.shape[2]
    tm = _tile(m, GATED_ROWS, 16)
    tn = _tile(f, 256, LANES)
    blocks = (_nbytes((tm, LANES), F32) + 2 * _nbytes((d, tn), w_gate.dtype) + _nbytes((tm, tn), BF16))
    w_spec = pl.BlockSpec((None, d, tn), lambda i, j: (layer, 0, j))
    return pl.pallas_call(
        _swiglu_body,
        grid=(m // tm, f // tn),
        in_specs=[_resident_rows_spec(tm, d),
                  pl.BlockSpec((tm, LANES), lambda i, j: (i, 0)), w_spec, w_spec],
        out_specs=pl.BlockSpec((tm, tn), lambda i, j: (i, j)),
        out_shape=jax.ShapeDtypeStruct((m, f), BF16),
        compiler_params=_params(("parallel", "parallel"), blocks, _nbytes((tm, d), BF16)),
        name="swiglu_up",
    )(xg, ss, w_gate, w_up)


def _glu(xg, ss, w, b, layer):
    m, d = xg.shape
    n = w.shape[2] // 2
    tm = _tile(m, GATED_ROWS, 16)
    tn = _tile(n, 256, LANES)
    nj = n // tn
    b2 = b.reshape(1, 2 * n)
    blocks = (_nbytes((tm, LANES), F32) + 2 * _nbytes((d, tn), w.dtype)
              + _nbytes((tm, tn), F32) + 2 * _nbytes((1, tn), F32))
    return pl.pallas_call(
        _glu_body,
        grid=(m // tm, nj),
        in_specs=[_resident_rows_spec(tm, d),
                  pl.BlockSpec((tm, LANES), lambda i, j: (i, 0)),
                  pl.BlockSpec((None, d, tn), lambda i, j: (layer, 0, j)),
                  pl.BlockSpec((None, d, tn), lambda i, j: (layer, 0, j + nj)),
                  pl.BlockSpec((1, tn), lambda i, j: (0, j)),
                  pl.BlockSpec((1, tn), lambda i, j: (0, j + nj))],
        out_specs=pl.BlockSpec((tm, tn), lambda i, j: (i, j)),
        out_shape=jax.ShapeDtypeStruct((m, n), F32),
        compiler_params=_params(("parallel", "parallel"), blocks, _nbytes((tm, d), BF16)),
        name="conv_pw1_glu",
    )(xg, ss, w, w, b2, b2)


def _residual_mm_body(*refs, scale, has_bias, emit_next):
    a_ref, w_ref = refs[0], refs[1]
    pos = 2
    y = _mxu_dot(a_ref[...], w_ref)
    if has_bias:
        y = y + refs[pos][...]
        pos += 1
    xn = refs[pos][...] + scale * y
    pos += 1
    if emit_next:
        gn_ref, o_ref, xg_ref, ss_ref = refs[pos:pos + 4]
        o_ref[...] = xn
        _emit_next(xn, gn_ref, xg_ref, ss_ref, pl.program_id(1))
    else:
        refs[pos][...] = xn


def _residual_mm(a, w, layer, x, *, scale, bias=None, g_next=None, tm_pref, tn_pref, name):
    m, k = a.shape
    n = w.shape[2]
    tm = _tile(m, tm_pref, 16)
    tn = _tile(n, tn_pref, LANES)
    emit_next = g_next is not None
    blocks = (_nbytes((tm, k), BF16) + _nbytes((k, tn), w.dtype) + 2 * _nbytes((tm, tn), F32)
              + 2 * _nbytes((1, tn), F32) + _nbytes((tm, tn), BF16) + _nbytes((tm, LANES), F32))
    v_spec = pl.BlockSpec((1, tn), lambda i, j: (0, j))
    x_spec = pl.BlockSpec((tm, tn), lambda i, j: (i, j))
    in_specs = [pl.BlockSpec((tm, k), lambda i, j: (i, 0)),
                pl.BlockSpec((None, k, tn), lambda i, j: (layer, 0, j))]
    args = [a, w]
    if bias is not None:
        in_specs.append(v_spec)
        args.append(bias.reshape(1, n))
    in_specs.append(x_spec)
    args.append(x)
    out_specs, out_shape = x_spec, jax.ShapeDtypeStruct((m, n), F32)
    if emit_next:
        in_specs.append(v_spec)
        args.append(g_next.reshape(1, n))
        out_specs = [x_spec, x_spec, pl.BlockSpec((tm, LANES), lambda i, j: (i, 0))]
        out_shape = [out_shape, jax.ShapeDtypeStruct((m, n), BF16),
                     jax.ShapeDtypeStruct((m, LANES), F32)]
    body = functools.partial(_residual_mm_body, scale=scale, has_bias=bias is not None,
                             emit_next=emit_next)
    return pl.pallas_call(
        body,
        grid=(m // tm, n // tn),
        in_specs=in_specs,
        out_specs=out_specs,
        out_shape=out_shape,
        compiler_params=_params(("parallel", "arbitrary"), blocks),
        name=name,
    )(*args)


def _ple_body(*refs, emit_next):
    xg_ref, ss_ref, wg_ref, p_ref, wp_ref, x_ref = refs[:6]
    xg = xg_ref[...]
    gate = jax.nn.sigmoid(_mxu_dot(xg, wg_ref) * _row_rsqrt(ss_ref, xg.shape[1]))
    proj = _mxu_dot(p_ref[...], wp_ref)
    xn = x_ref[...] + proj * gate
    if emit_next:
        gn_ref, o_ref, xg_out_ref, ss_out_ref = refs[6:10]
        o_ref[...] = xn
        _emit_next(xn, gn_ref, xg_out_ref, ss_out_ref, pl.program_id(1))
    else:
        refs[6][...] = xn


def _ple(xg, ss, w_gate, p, w_proj, layer, x, g_next=None):
    m, d = xg.shape
    e = p.shape[1]
    n = w_gate.shape[2]
    tm = _tile(m, 1040, 16)
    tn = _tile(n, 512, LANES)
    emit_next = g_next is not None
    blocks = (_nbytes((tm, d), BF16) + 2 * _nbytes((tm, LANES), F32) + _nbytes((d, tn), w_gate.dtype)
              + _nbytes((tm, e), BF16) + _nbytes((e, tn), w_proj.dtype) + 2 * _nbytes((tm, tn), F32)
              + _nbytes((tm, tn), BF16) + _nbytes((1, tn), F32))
    x_spec = pl.BlockSpec((tm, tn), lambda i, j: (i, j))
    ss_spec = pl.BlockSpec((tm, LANES), lambda i, j: (i, 0))
    in_specs = [pl.BlockSpec((tm, d), lambda i, j: (i, 0)), ss_spec,
                pl.BlockSpec((None, d, tn), lambda i, j: (layer, 0, j)),
                pl.BlockSpec((tm, e), lambda i, j: (i, 0)),
                pl.BlockSpec((None, e, tn), lambda i, j: (layer, 0, j)),
                x_spec]
    args = [xg, ss, w_gate, p, w_proj, x]
    out_specs, out_shape = x_spec, jax.ShapeDtypeStruct((m, n), F32)
    if emit_next:
        in_specs.append(pl.BlockSpec((1, tn), lambda i, j: (0, j)))
        args.append(g_next.reshape(1, n))
        out_specs = [x_spec, x_spec, ss_spec]
        out_shape = [out_shape, jax.ShapeDtypeStruct((m, n), BF16),
                     jax.ShapeDtypeStruct((m, LANES), F32)]
    return pl.pallas_call(
        functools.partial(_ple_body, emit_next=emit_next),
        grid=(m // tm, n // tn),
        in_specs=in_specs,
        out_specs=out_specs,
        out_shape=out_shape,
        compiler_params=_params(("parallel", "arbitrary"), blocks),
        name="ple",
    )(*args)


class _Streams:
    def __init__(self, row0, n_seq, seq_len, t_blk, halo):
        assert row0 % t_blk == 0 and seq_len % t_blk == 0
        self.row0, self.n_seq, self.seq_len, self.t_blk, self.halo = row0, n_seq, seq_len, t_blk, halo
        self.blocks_per_seq = seq_len // t_blk

    def row_block(self, s, i):
        return (self.row0 // self.t_blk + s * self.blocks_per_seq + i, 0)

    def prev_rows_block(self, s, i):
        per_blk = self.t_blk // self.halo
        return (self.row0 // self.halo + s * (self.seq_len // self.halo)
                + jnp.maximum(i * per_blk - 1, 0), 0)


def _share_outputs(in_specs, args, fill, n_shared):
    aliases = {len(args) + k: k for k in range(n_shared)}
    return (in_specs + [pl.BlockSpec(memory_space=pl.ANY)] * n_shared,
            args + list(fill[:n_shared]), aliases)


def _pool_body(x_ref, halo_ref, g_ref, w_ref, b_ref, sc_ref, gn_ref, *rest,
               t_blk, start_pos, halo_is_state, n_prev):
    o_ref, xg_ref, ss_ref, st_ref, ext_ref = rest[n_prev:]
    i = pl.program_id(1)
    grp = w_ref.shape[1]
    g = g_ref[...]
    ext_ref[pl.ds(POOL_HALO, t_blk), :] = _rms(x_ref[...]) * g
    if halo_is_state:
        ext_ref[pl.ds(0, POOL_HALO), :] = halo_ref[0]
    else:
        ext_ref[pl.ds(0, POOL_HALO), :] = jnp.where(i > 0, _rms(halo_ref[...]) * g, 0.0)
    pos = start_pos + i * t_blk + lax.broadcasted_iota(jnp.int32, (t_blk, 1), 0)
    for gi, win in enumerate(POOL_WINDOWS):
        cols = pl.ds(gi * grp, grp)
        h = ext_ref[pl.ds(POOL_HALO, t_blk), cols]
        s = h
        for k in range(1, win):
            s = s + ext_ref[pl.ds(POOL_HALO - k, t_blk), cols]
        cnt = jnp.minimum(pos + 1, win).astype(F32)
        d = (s / cnt - h).astype(BF16)
        y = jnp.dot(d, w_ref[gi], preferred_element_type=F32) + b_ref[gi]
        o_ref[:, cols] = x_ref[:, cols] + y * sc_ref[:, cols]
    xn = o_ref[...]
    xg_ref[...] = (xn * gn_ref[...]).astype(BF16)
    ss_ref[...] = _lane_partial_sumsq(xn)

    @pl.when(i == pl.num_programs(1) - 1)
    def _():
        st_ref[0] = ext_ref[pl.ds(t_blk, POOL_HALO), :]


def _pool_mixer(x, streams, state, start_pos, g, w, layer, b, scale, g_next, prev_outs):
    m, d = x.shape
    n_grp, grp = w.shape[1], w.shape[2]
    t_blk = streams.t_blk
    if state is None:
        halo, halo_spec = x, pl.BlockSpec((POOL_HALO, d), streams.prev_rows_block)
    else:
        halo, halo_spec = state, pl.BlockSpec((1, POOL_HALO, d), lambda s, i: (s, 0, 0))
    blocks = (2 * _nbytes((t_blk, d), F32) + 2 * _nbytes((POOL_HALO, d), F32)
              + _nbytes(w.shape[1:], w.dtype) + 4 * _nbytes((1, d), F32)
              + _nbytes((t_blk, d), BF16) + _nbytes((t_blk, LANES), F32))
    scratch = _nbytes((POOL_HALO + t_blk, d), F32)
    n_prev = 3
    body = functools.partial(_pool_body, t_blk=t_blk, start_pos=start_pos,
                             halo_is_state=state is not None, n_prev=n_prev)
    vec = pl.BlockSpec((1, d), lambda s, i: (0, 0))
    row_spec = pl.BlockSpec((t_blk, d), streams.row_block)
    in_specs, args, aliases = _share_outputs(
        [row_spec, halo_spec, vec,
         pl.BlockSpec((None, n_grp, grp, grp), lambda s, i: (layer, 0, 0, 0)),
         pl.BlockSpec((n_grp, 1, grp), lambda s, i: (0, 0, 0)), vec, vec],
        [x, halo, g.reshape(1, d), w, b.reshape(n_grp, 1, grp), scale.reshape(1, d),
         g_next.reshape(1, d)],
        prev_outs, n_prev)
    return pl.pallas_call(
        body,
        grid=(streams.n_seq, streams.blocks_per_seq),
        in_specs=in_specs,
        out_specs=[row_spec, row_spec, pl.BlockSpec((t_blk, LANES), streams.row_block),
                   pl.BlockSpec((1, POOL_HALO, d), lambda s, i: (s, 0, 0))],
        out_shape=[jax.ShapeDtypeStruct((m, d), F32), jax.ShapeDtypeStruct((m, d), BF16),
                   jax.ShapeDtypeStruct((m, LANES), F32),
                   jax.ShapeDtypeStruct((streams.n_seq, POOL_HALO, d), F32)],
        scratch_shapes=[pltpu.VMEM((POOL_HALO + t_blk, d), F32)],
        input_output_aliases=aliases,
        compiler_params=_params(("parallel", "arbitrary"), blocks, scratch),
        name="pool_mixer",
    )(*args)


def _dwconv_body(u_ref, halo_ref, w_ref, bdw_ref, lng_ref, lnb_ref, *rest,
                 t_blk, halo_is_state, n_prev):
    o_ref, ext_ref, c_ref = rest[n_prev:]
    i = pl.program_id(1)
    width = w_ref.shape[0]
    d = u_ref.shape[-1]
    ext_ref[pl.ds(CONV_HALO, t_blk), :] = u_ref[...]
    if halo_is_state:
        ext_ref[pl.ds(0, CONV_HALO), :] = halo_ref[0]
    else:
        ext_ref[pl.ds(0, CONV_HALO), :] = jnp.where(i > 0, halo_ref[...], 0.0)
    first = CONV_HALO - (width - 1)
    cw = min(CONV_COL_CHUNK, d)
    for c0 in range(0, d, cw):
        cols = pl.ds(c0, cw)
        acc = None
        for r in range(SUBLANES):
            taps = [k for k in range(width) if (first + k) % SUBLANES == r]
            if not taps:
                continue
            rows = t_blk + (SUBLANES if r else 0)
            part = None
            for k in taps:
                term = w_ref[pl.ds(k, 1), cols] * ext_ref[pl.ds(first + k - r, rows), cols]
                part = term if part is None else part + term
            part = part[r:r + t_blk]
            acc = part if acc is None else acc + part
        c_ref[:, cols] = acc + bdw_ref[:, cols]
    c = c_ref[...]
    cc = c - jnp.mean(c, axis=-1, keepdims=True)
    var = jnp.mean(cc * cc, axis=-1, keepdims=True)
    y = cc * lax.rsqrt(var + EPS) * lng_ref[...] + lnb_ref[...]
    o_ref[...] = jax.nn.silu(y).astype(o_ref.dtype)


def _dwconv_ln_swish(u, streams, state, w_dw, b_dw, ln_g, ln_b, prev_outs):
    m, d = u.shape
    width = w_dw.shape[0]
    assert width - 1 <= CONV_HALO
    t_blk = streams.t_blk
    if state is None:
        halo, halo_spec = u, pl.BlockSpec((CONV_HALO, d), streams.prev_rows_block)
    else:
        halo, halo_spec = state, pl.BlockSpec((1, CONV_HALO, d), lambda s, i: (s, 0, 0))
    blocks = (_nbytes((t_blk, d), F32) + _nbytes((CONV_HALO, d), F32) + _nbytes((t_blk, d), BF16)
              + _nbytes((width, d), F32) + 3 * _nbytes((1, d), F32))
    scratch = _nbytes((CONV_HALO + t_blk, d), F32) + _nbytes((t_blk, d), F32)
    n_prev = 1
    body = functools.partial(_dwconv_body, t_blk=t_blk, halo_is_state=state is not None,
                             n_prev=n_prev)
    vec = pl.BlockSpec((1, d), lambda s, i: (0, 0))
    row_spec = pl.BlockSpec((t_blk, d), streams.row_block)
    in_specs, args, aliases = _share_outputs(
        [row_spec, halo_spec, pl.BlockSpec((width, d), lambda s, i: (0, 0)), vec, vec, vec],
        [u, halo, w_dw, b_dw.reshape(1, d), ln_g.reshape(1, d), ln_b.reshape(1, d)],
        prev_outs, n_prev)
    return pl.pallas_call(
        body,
        grid=(streams.n_seq, streams.blocks_per_seq),
        in_specs=in_specs,
        out_specs=[row_spec],
        out_shape=[jax.ShapeDtypeStruct((m, d), BF16)],
        scratch_shapes=[pltpu.VMEM((CONV_HALO + t_blk, d), F32),
                        pltpu.VMEM((t_blk, d), F32)],
        input_output_aliases=aliases,
        compiler_params=_params(("parallel", "parallel"), blocks, scratch),
        name="dwconv_ln_swish",
    )(*args)


def _pad_rows_front(state, rows):
    return jnp.pad(state, ((0, 0), (rows - state.shape[1], 0), (0, 0)))


def _ffn(xg, ss, x, w_gate, w_up, w_down, layer, g_next):
    a = _swiglu_up(xg, ss, w_gate, w_up, layer)
    return _residual_mm(a, w_down, layer, x, scale=0.5, g_next=g_next,
                        tm_pref=640, tn_pref=256, name="ffn_down")


def _trunk(x_prompt, x_sample, state_pool, state_conv, p_prompt, p_sample, W):
    bp, n_p, d = x_prompt.shape
    bs, n_s, _ = x_sample.shape
    mp, ms = bp * n_p, bs * n_s
    depth = W['g_ffn1'].shape[0]
    x = jnp.concatenate([x_prompt.reshape(mp, d), x_sample.reshape(ms, d)], axis=0)
    p = jnp.concatenate([p_prompt.reshape(depth, mp, -1), p_sample.reshape(depth, ms, -1)],
                        axis=1).astype(BF16)

    def groups(prompt_blk, halo):
        return (_Streams(0, bp, n_p, _tile(n_p, prompt_blk, halo), halo),
                _Streams(mp, bs, n_s, _tile(n_s, prompt_blk, halo) if n_s >= halo else n_s, halo))

    new_pool_p, new_pool_s, new_conv_p, new_conv_s = [], [], [], []
    xg, ss = _prep(x, W['g_ffn1'][0])
    for i in range(depth):
        j = i // 2
        if i % 2 == 0:
            dead = (x, xg, ss)
            x = _ffn(xg, ss, x, W['ffn1_w_gate'], W['ffn1_w_up'], W['ffn1_w_down'], i, None)
            sp, s_s = groups(256, POOL_HALO)
            pool_args = (W['g_mix'][i], W['pool_w'], j, W['pool_b'][j], W['pool_scale'][j],
                         W['g_ffn2'][i])
            outs = _pool_mixer(x, sp, None, 0, *pool_args, dead)
            new_pool_p.append(outs[3][:, POOL_HALO - POOL_HIST:])
            outs = _pool_mixer(x, s_s, _pad_rows_front(state_pool[j], POOL_HALO), SAMPLE_START_POS,
                               *pool_args, outs)
            new_pool_s.append(outs[3][:, POOL_HALO - POOL_HIST:])
            x, xg, ss = outs[:3]
        else:
            x, xg, ss = _ffn(xg, ss, x, W['ffn1_w_gate'], W['ffn1_w_up'], W['ffn1_w_down'], i,
                             W['g_mix'][i])
            u = _glu(xg, ss, W['conv_w_pw1'], W['conv_b_pw1'][j], j)
            hist = W['conv_w_dw'].shape[1] - 1
            sp, s_s = groups(128, CONV_HALO)
            conv_args = (W['conv_w_dw'][j], W['conv_b_dw'][j], W['conv_ln_g'][j], W['conv_ln_b'][j])
            outs = _dwconv_ln_swish(u, sp, None, *conv_args, (xg,))
            outs = _dwconv_ln_swish(u, s_s, _pad_rows_front(state_conv[j], CONV_HALO), *conv_args, outs)
            new_conv_p.append(jnp.stack([u[(s + 1) * n_p - hist:(s + 1) * n_p] for s in range(bp)]))
            new_conv_s.append(jnp.concatenate([state_conv[j], u[mp:].reshape(bs, n_s, d)],
                                              axis=1)[:, -hist:])
            x, xg, ss = _residual_mm(outs[0], W['conv_w_pw2'], j, x, scale=1.0,
                                     bias=W['conv_b_pw2'][j], g_next=W['g_ffn2'][i],
                                     tm_pref=1040, tn_pref=512, name="conv_pw2")
        x, xg, ss = _ffn(xg, ss, x, W['ffn2_w_gate'], W['ffn2_w_up'], W['ffn2_w_down'], i,
                         W['g_ple'][i])
        if i + 1 < depth:
            x, xg, ss = _ple(xg, ss, W['ple_w_gate'], p[i], W['ple_w_proj'], i, x, W['g_ffn1'][i + 1])
        else:
            x = _ple(xg, ss, W['ple_w_gate'], p[i], W['ple_w_proj'], i, x)
    y_prompt = _rmsnorm_rows(x, W['g_final'], 0, mp).reshape(bp, n_p, d)
    y_sample = _rmsnorm_rows(x, W['g_final'], mp, ms).reshape(bs, n_s, d)
    return (y_prompt, y_sample, jnp.stack(new_pool_p), jnp.stack(new_conv_p),
            jnp.stack(new_pool_s), jnp.stack(new_conv_s))


_PRECAST_WEIGHTS = ('ffn1_w_down', 'ffn2_w_down', 'pool_w')


def kernel(x_prompt, x_sample, state_pool, state_conv, p_prompt, p_sample, g_ffn1, ffn1_w_gate, ffn1_w_up, ffn1_w_down, g_mix, pool_w, pool_b, pool_scale, conv_w_pw1, conv_b_pw1, conv_w_dw, conv_b_dw, conv_ln_g, conv_ln_b, conv_w_pw2, conv_b_pw2, g_ffn2, ffn2_w_gate, ffn2_w_up, ffn2_w_down, g_ple, ple_w_gate, ple_w_proj, g_final):
    W = {
        'g_ffn1': g_ffn1, 'ffn1_w_gate': ffn1_w_gate, 'ffn1_w_up': ffn1_w_up, 'ffn1_w_down': ffn1_w_down,
        'g_mix': g_mix, 'pool_w': pool_w, 'pool_b': pool_b, 'pool_scale': pool_scale,
        'conv_w_pw1': conv_w_pw1, 'conv_b_pw1': conv_b_pw1, 'conv_w_dw': conv_w_dw, 'conv_b_dw': conv_b_dw,
        'conv_ln_g': conv_ln_g, 'conv_ln_b': conv_ln_b, 'conv_w_pw2': conv_w_pw2, 'conv_b_pw2': conv_b_pw2,
        'g_ffn2': g_ffn2, 'ffn2_w_gate': ffn2_w_gate, 'ffn2_w_up': ffn2_w_up, 'ffn2_w_down': ffn2_w_down,
        'g_ple': g_ple, 'ple_w_gate': ple_w_gate, 'ple_w_proj': ple_w_proj, 'g_final': g_final,
    }
    for name in _PRECAST_WEIGHTS:
        W[name] = W[name].astype(BF16)
    return _trunk(x_prompt, x_sample, state_pool, state_conv, p_prompt, p_sample, W)
```
